```python
import math
import jax, jax.numpy as jnp
from jax import lax
import numpy as np

D_MODEL = 1024
BATCH = 8
SEQ = 4096
DEPTH = 2

F32 = jnp.float32
EPS = 1e-6
NEG_INF = -1e30
ROPE_THETA = 10000.0
BLOCK = 128
HEAD_DIM = 64

SWA_HEADS = 8
SWA_KV_HEADS = 2
SWA_WINDOW = 128
S5_CHANNELS = 512
S5_GROUP = 16
S5_GROUPS = S5_CHANNELS // S5_GROUP
S5_STATE = 64
S5_DT_MIN = 1e-3
S5_DT_MAX = 1e-1
RET_HEADS = 4
RET_QK_DIM = 64
RET_V_DIM = 128
RET_CHUNK = 128
MLA_HEADS = 8
MLA_Q_RANK = 256
MLA_KV_RANK = 128
MLA_NOPE_DIM = 64
MLA_ROPE_DIM = 32
MLA_V_DIM = 64
N_BRANCH = 4
BRANCH_WIDTH = 512
D_FF = 4 * D_MODEL

SWA_Q_W = SWA_HEADS * HEAD_DIM
SWA_KV_W = SWA_KV_HEADS * HEAD_DIM
RET_QK_W = RET_HEADS * RET_QK_DIM
RET_V_W = RET_HEADS * RET_V_DIM
IN_SPLITS = (SWA_Q_W, SWA_KV_W, SWA_KV_W,
             S5_CHANNELS,
             RET_QK_W, RET_QK_W, RET_V_W, RET_V_W,
             MLA_Q_RANK, MLA_KV_RANK, MLA_ROPE_DIM,
             N_BRANCH * D_MODEL)
D_IN = sum(IN_SPLITS)

kernel_name = "hybrid_gated_swa_s5_retnet_mla_block"


def rms_norm(x, g):
    xf = x.astype(F32)
    y = xf * lax.rsqrt(jnp.mean(xf * xf, axis=-1, keepdims=True) + EPS)
    return (y * g.astype(F32)).astype(x.dtype)


def rope_tables(seq, dim):
    inv = 1.0 / (ROPE_THETA ** (jnp.arange(0, dim, 2, dtype=F32) / dim))
    ang = jnp.arange(seq, dtype=F32)[:, None] * inv[None, :]
    return jnp.cos(ang), jnp.sin(ang)


def apply_rope(x, cos, sin):
    xf = x.astype(F32)
    x1, x2 = jnp.split(xf, 2, axis=-1)
    c = cos[None, :, None, :]
    s = sin[None, :, None, :]
    return jnp.concatenate([x1 * c - x2 * s, x2 * c + x1 * s], axis=-1).astype(x.dtype)


def swa_attention(q, k, v, sinks):
    b, s_len, h, d = q.shape
    nb = s_len // BLOCK
    grp = h // SWA_KV_HEADS
    qb = q.reshape(b, nb, BLOCK, SWA_KV_HEADS, grp, d)

    def with_prev(t):
        tb = t.reshape(b, nb, BLOCK, SWA_KV_HEADS, d)
        prev = jnp.pad(tb[:, :-1], ((0, 0), (1, 0), (0, 0), (0, 0), (0, 0)))
        return jnp.concatenate([prev, tb], axis=2)

    kb, vb = with_prev(k), with_prev(v)
    sc = jnp.einsum('bnqkgd,bnckd->bnkgqc', qb, kb, preferred_element_type=F32) * (d ** -0.5)
    qi = jnp.arange(BLOCK)[:, None] + BLOCK
    kj = jnp.arange(2 * BLOCK)[None, :]
    diff = qi - kj
    band = (diff >= 0) & (diff < SWA_WINDOW)
    has_prev = (jnp.arange(nb) > 0)[:, None, None] | (kj >= BLOCK)[None]
    mask = (band[None] & has_prev)[None, :, None, None]
    sc = jnp.where(mask, sc, NEG_INF)
    sink = sinks.astype(F32).reshape(1, 1, SWA_KV_HEADS, grp, 1, 1)
    m = jnp.maximum(jnp.max(sc, axis=-1, keepdims=True), sink)
    p = jnp.exp(sc - m)
    denom = jnp.sum(p, axis=-1, keepdims=True) + jnp.exp(sink - m)
    o = jnp.einsum('bnkgqc,bnckd->bnqkgd', (p / denom).astype(v.dtype), vb)
    return o.reshape(b, s_len, h * d)


def s5_ssm(u, lam_re, lam_im, log_dt, b_re, b_im, c_re, c_im, d_skip, w_glu):
    b, s_len, _ = u.shape
    uf = u.astype(F32).reshape(b, s_len, S5_GROUPS, S5_GROUP)
    dt = jnp.exp(log_dt.astype(F32))[:, None]
    lr, li = lam_re.astype(F32), lam_im.astype(F32)
    mag = jnp.exp(lr * dt)
    ab_re, ab_im = mag * jnp.cos(li * dt), mag * jnp.sin(li * dt)
    den = lr * lr + li * li
    nr, ni = ab_re - 1.0, ab_im
    f_re = (nr * lr + ni * li) / den
    f_im = (ni * lr - nr * li) / den
    br, bi = b_re.astype(F32), b_im.astype(F32)
    bb_re = f_re[..., None] * br - f_im[..., None] * bi
    bb_im = f_re[..., None] * bi + f_im[..., None] * br
    bu_re = jnp.einsum('bsgc,gpc->bsgp', uf, bb_re)
    bu_im = jnp.einsum('bsgc,gpc->bsgp', uf, bb_im)
    a_re = jnp.broadcast_to(ab_re, bu_re.shape)
    a_im = jnp.broadcast_to(ab_im, bu_im.shape)

    def combine(left, right):
        a1r, a1i, b1r, b1i = left
        a2r, a2i, b2r, b2i = right
        return (a1r * a2r - a1i * a2i, a1r * a2i + a1i * a2r,
                a2r * b1r - a2i * b1i + b2r, a2r * b1i + a2i * b1r + b2i)

    _, _, x_re, x_im = lax.associative_scan(combine, (a_re, a_im, bu_re, bu_im), axis=1)
    y = (jnp.einsum('bsgp,gcp->bsgc', x_re, c_re.astype(F32))
         - jnp.einsum('bsgp,gcp->bsgc', x_im, c_im.astype(F32)))
    y = y.reshape(b, s_len, S5_CHANNELS) + d_skip.astype(F32) * u.astype(F32)
    z = jax.nn.gelu(y).astype(u.dtype)
    za, zb = jnp.split(z @ w_glu, 2, axis=-1)
    return za * jax.nn.sigmoid(zb)


def retention(q, k, v, g, cos, sin):
    b, s_len, h, dk = q.shape
    dv = v.shape[-1]
    c = RET_CHUNK
    nc = s_len // c
    qf = apply_rope(q, cos, sin).astype(F32)
    kf = apply_rope(k, cos, sin).astype(F32) * (dk ** -0.5)
    qc = qf.reshape(b, nc, c, h, dk)
    kc = kf.reshape(b, nc, c, h, dk)
    vc = v.astype(F32).reshape(b, nc, c, h, dv)
    log_gamma = jnp.log1p(-jnp.exp2(-5.0 - jnp.arange(h, dtype=F32)))
    idx = jnp.arange(c, dtype=F32)
    diff = idx[:, None] - idx[None, :]
    decay = jnp.where(diff >= 0, jnp.exp(log_gamma[:, None, None] * jnp.maximum(diff, 0.0)), 0.0)
    inner_s = jnp.einsum('bnqhd,bnkhd->bnhqk', qc, kc) * decay
    inner = jnp.einsum('bnhqk,bnkhe->bnqhe', inner_s, vc)
    k_w = jnp.exp(log_gamma[None, :] * (c - 1 - idx)[:, None])
    upd = jnp.einsum('bnkhd,bnkhe->bnhde', kc * k_w[None, None, :, :, None], vc)
    chunk_decay = jnp.exp(log_gamma * c)[None, :, None, None]

    def step(state, u_n):
        return state * chunk_decay + u_n, state

    _, prev = lax.scan(step, jnp.zeros((b, h, dk, dv), F32), jnp.moveaxis(upd, 1, 0))
    prev = jnp.moveaxis(prev, 0, 1)
    q_w = jnp.exp(log_gamma[None, :] * (idx + 1.0)[:, None])
    cross = jnp.einsum('bnqhd,bnhde->bnqhe', qc * q_w[None, None, :, :, None], prev)
    y = (inner + cross).reshape(b, s_len, h, dv)
    mu = jnp.mean(y, axis=-1, keepdims=True)
    var = jnp.mean(jnp.square(y - mu), axis=-1, keepdims=True)
    y = ((y - mu) * lax.rsqrt(var + EPS)).reshape(b, s_len, h * dv)
    return (jax.nn.silu(g.astype(F32)) * y).astype(v.dtype)


def mla_attention(c_q, c_kv, k_rope, g_q, g_kv, w_uq, w_ukv, cos, sin):
    b, s_len, _ = c_q.shape
    h = MLA_HEADS
    q = (rms_norm(c_q, g_q) @ w_uq).reshape(b, s_len, h, MLA_NOPE_DIM + MLA_ROPE_DIM)
    q_nope = q[..., :MLA_NOPE_DIM]
    q_rope = apply_rope(q[..., MLA_NOPE_DIM:], cos, sin)
    kv = (rms_norm(c_kv, g_kv) @ w_ukv).reshape(b, s_len, h, MLA_NOPE_DIM + MLA_V_DIM)
    k_nope, v = kv[..., :MLA_NOPE_DIM], kv[..., MLA_NOPE_DIM:]
    k_r = apply_rope(k_rope[:, :, None, :], cos, sin)[:, :, 0]
    scale = (MLA_NOPE_DIM + MLA_ROPE_DIM) ** -0.5
    nb = s_len // BLOCK
    qn_b = jnp.moveaxis(q_nope.reshape(b, nb, BLOCK, h, MLA_NOPE_DIM), 1, 0)
    qr_b = jnp.moveaxis(q_rope.reshape(b, nb, BLOCK, h, MLA_ROPE_DIM), 1, 0)
    kpos = jnp.arange(s_len)

    def one_block(args):
        qn, qr, i = args
        sc = (jnp.einsum('bqhd,bkhd->bhqk', qn, k_nope, preferred_element_type=F32)
              + jnp.einsum('bqhd,bkd->bhqk', qr, k_r, preferred_element_type=F32)) * scale
        qpos = i * BLOCK + jnp.arange(BLOCK)
        sc = jnp.where(kpos[None, :] <= qpos[:, None], sc, NEG_INF)
        p = jax.nn.softmax(sc, axis=-1).astype(v.dtype)
        return jnp.einsum('bhqk,bkhe->bqhe', p, v)

    o = lax.map(one_block, (qn_b, qr_b, jnp.arange(nb)))
    return jnp.moveaxis(o, 0, 1).reshape(b, s_len, h * MLA_V_DIM)


def setup_inputs(seed: int = 0) -> dict:
    key = jax.random.key(seed)
    ks = jax.random.split(key, 32)
    L = DEPTH

    def nrm(k, shape, scale):
        return jax.random.normal(k, shape, F32) * scale

    def gain(k, shape):
        return 1.0 + 0.05 * jax.random.normal(k, shape, F32)

    n_idx = jnp.arange(S5_STATE, dtype=F32)
    return {
        "x": jax.random.normal(ks[0], (BATCH, SEQ, D_MODEL), F32),
        "g_pre_mix": gain(ks[1], (L, D_MODEL)),
        "g_post_mix": gain(ks[2], (L, D_MODEL)),
        "g_pre_mlp": gain(ks[3], (L, D_MODEL)),
        "g_post_mlp": gain(ks[4], (L, D_MODEL)),
        "w_in": nrm(ks[5], (L, D_MODEL, D_IN), D_MODEL ** -0.5),
        "swa_sinks": nrm(ks[6], (L, SWA_HEADS), 0.5),
        "s5_lam_re": -0.5 + 0.01 * jax.random.normal(ks[7], (L, S5_GROUPS, S5_STATE), F32),
        "s5_lam_im": math.pi * n_idx + 0.01 * jax.random.normal(ks[8], (L, S5_GROUPS, S5_STATE), F32),
        "s5_log_dt": jax.random.uniform(ks[9], (L, S5_GROUPS), F32, math.log(S5_DT_MIN), math.log(S5_DT_MAX)),
        "s5_b_re": nrm(ks[10], (L, S5_GROUPS, S5_STATE, S5_GROUP), (2 * S5_GROUP) ** -0.5),
        "s5_b_im": nrm(ks[11], (L, S5_GROUPS, S5_STATE, S5_GROUP), (2 * S5_GROUP) ** -0.5),
        "s5_c_re": nrm(ks[12], (L, S5_GROUPS, S5_GROUP, S5_STATE), (2 * S5_STATE) ** -0.5),
        "s5_c_im": nrm(ks[13], (L, S5_GROUPS, S5_GROUP, S5_STATE), (2 * S5_STATE) ** -0.5),
        "s5_d": nrm(ks[14], (L, S5_CHANNELS), 1.0),
        "s5_w_glu": nrm(ks[15], (L, S5_CHANNELS, 2 * S5_CHANNELS), S5_CHANNELS ** -0.5),
        "mla_g_q": gain(ks[16], (L, MLA_Q_RANK)),
        "mla_g_kv": gain(ks[17], (L, MLA_KV_RANK)),
        "mla_w_uq": nrm(ks[18], (L, MLA_Q_RANK, MLA_HEADS * (MLA_NOPE_DIM + MLA_ROPE_DIM)), MLA_Q_RANK ** -0.5),
        "mla_w_ukv": nrm(ks[19], (L, MLA_KV_RANK, MLA_HEADS * (MLA_NOPE_DIM + MLA_V_DIM)), MLA_KV_RANK ** -0.5),
        "w_branch": nrm(ks[20], (L, N_BRANCH, BRANCH_WIDTH, D_MODEL), BRANCH_WIDTH ** -0.5),
        "w_out": nrm(ks[21], (L, D_MODEL, D_MODEL), D_MODEL ** -0.5),
        "w_ff1": nrm(ks[22], (L, D_MODEL, D_FF), D_MODEL ** -0.5),
        "w_ff2": nrm(ks[23], (L, D_FF, D_MODEL), D_FF ** -0.5),
    }


def reference(x, g_pre_mix, g_post_mix, g_pre_mlp, g_post_mlp, w_in, swa_sinks,
              s5_lam_re, s5_lam_im, s5_log_dt, s5_b_re, s5_b_im, s5_c_re, s5_c_im,
              s5_d, s5_w_glu, mla_g_q, mla_g_kv, mla_w_uq, mla_w_ukv,
              w_branch, w_out, w_ff1, w_ff2):
    b, s_len, _ = x.shape
    cos_h, sin_h = rope_tables(s_len, HEAD_DIM)
    cos_r, sin_r = rope_tables(s_len, MLA_ROPE_DIM)
    split_at = tuple(int(i) for i in np.cumsum(IN_SPLITS)[:-1])
    for l in range(DEPTH):
        h = rms_norm(x, g_pre_mix[l])
        (sq, sk, sv, su, rq, rk, rv, rg, cq, ckv, kr, gate_logits) = jnp.split(h @ w_in[l], split_at, axis=-1)
        y_a = swa_attention(
            apply_rope(sq.reshape(b, s_len, SWA_HEADS, HEAD_DIM), cos_h, sin_h),
            apply_rope(sk.reshape(b, s_len, SWA_KV_HEADS, HEAD_DIM), cos_h, sin_h),
            sv.reshape(b, s_len, SWA_KV_HEADS, HEAD_DIM), swa_sinks[l])
        y_b = s5_ssm(su, s5_lam_re[l], s5_lam_im[l], s5_log_dt[l], s5_b_re[l], s5_b_im[l],
                     s5_c_re[l], s5_c_im[l], s5_d[l], s5_w_glu[l])
        y_c = retention(rq.reshape(b, s_len, RET_HEADS, RET_QK_DIM),
                        rk.reshape(b, s_len, RET_HEADS, RET_QK_DIM),
                        rv.reshape(b, s_len, RET_HEADS, RET_V_DIM), rg, cos_h, sin_h)
        y_d = mla_attention(cq, ckv, kr, mla_g_q[l], mla_g_kv[l], mla_w_uq[l], mla_w_ukv[l], cos_r, sin_r)
        gates = jax.nn.sigmoid(gate_logits.astype(F32)).astype(x.dtype).reshape(b, s_len, N_BRANCH, D_MODEL)
        merged = (gates[:, :, 0] * (y_a @ w_branch[l, 0])
                  + gates[:, :, 1] * (y_b @ w_branch[l, 1])
                  + gates[:, :, 2] * (y_c @ w_branch[l, 2])
                  + gates[:, :, 3] * (y_d @ w_branch[l, 3]))
        x = x + rms_norm(merged @ w_out[l], g_post_mix[l])
        h = rms_norm(x, g_pre_mlp[l])
        f = jnp.square(jax.nn.relu(h @ w_ff1[l])) @ w_ff2[l]
        x = x + rms_norm(f, g_post_mlp[l])
    return x
```

```python
import functools
import math

import jax
import jax.numpy as jnp
import numpy as np
from jax import lax
from jax.experimental import pallas as pl
from jax.experimental.pallas import tpu as pltpu

F32 = jnp.float32
BF16 = jnp.bfloat16

D_MODEL = 1024
DEPTH = 2
EPS = 1e-6
NEG_INF = -1e30
ROPE_THETA = 10000.0
BLOCK = 128
HEAD_DIM = 64
SWA_HEADS = 8
SWA_KV_HEADS = 2
SWA_GROUP = SWA_HEADS // SWA_KV_HEADS
S5_CHANNELS = 512
S5_GROUP = 16
S5_GROUPS = S5_CHANNELS // S5_GROUP
S5_STATE = 64
S5_STATES = S5_GROUPS * S5_STATE
RET_HEADS = 4
RET_QK_DIM = 64
RET_V_DIM = 128
RET_CHUNK = 128
MLA_HEADS = 8
MLA_Q_RANK = 256
MLA_KV_RANK = 128
MLA_NOPE_DIM = 64
MLA_ROPE_DIM = 32
MLA_V_DIM = 64
MLA_HEAD_PAD = 128
N_BRANCH = 4
BRANCH_WIDTH = 512
D_FF = 4 * D_MODEL

LANES = 128
VMEM_LIMIT_BYTES = 56 * 1024 * 1024

C_SWA_Q = 0
C_SWA_K = 512
C_RET_Q = 640
C_RET_K = 896
C_MLA_KR = 1152
N_ROPED = 1280
C_SWA_V = 1280
C_S5_U = 1408
C_RET_V = 1920
C_RET_G = 2432
C_MLA_CQ = 2944
C_MLA_CKV = 3200
N_IN = 3328
N_ROPED_64 = C_MLA_KR


def _dot(a, b):
    return jnp.dot(a, b, preferred_element_type=F32)


def _dot_nt(a, b):
    return lax.dot_general(a, b, (((1,), (1,)), ((), ())), preferred_element_type=F32)


def _dot_tn(a, b):
    return lax.dot_general(a, b, (((0,), (0,)), ((), ())), preferred_element_type=F32)


def _rms(x, g):
    return x * lax.rsqrt(jnp.mean(x * x, axis=-1, keepdims=True) + EPS) * g


def _sigmoid(x):
    return 1.0 / (1.0 + jnp.exp(-x))


def _gelu_tanh(x):
    return x * (0.5 * (1.0 + jnp.tanh(math.sqrt(2.0 / math.pi) * (x + 0.044715 * (x * x * x)))))


def _const_spec(shape):
    zeros = (0,) * len(shape)
    return pl.BlockSpec(shape, lambda *_: zeros)


def _inproj_kernel(x_ref, g_ref, w_ref, wr_ref, cos_ref, sin_ref, cosr_ref, sinr_ref, cosq_ref, sinq_ref,
                   gq_ref, gkv_ref, wuq_ref, wuqr_ref, wk_ref, wv_ref,
                   swa_ref, s5_ref, ret_ref, mq_ref, mk_ref, mv_ref):
    x = x_ref[...]
    h = _rms(x, g_ref[...]).astype(BF16)
    cos = cos_ref[...]
    sin = sin_ref[...]

    def roped(c0, width):
        z = _dot(h, w_ref[:, c0:c0 + width])
        zr = _dot(h, wr_ref[:, c0:c0 + width])
        reps = width // LANES
        return z * jnp.tile(cos, (1, reps)) + zr * jnp.tile(sin, (1, reps))

    def plain(c0, width):
        return _dot(h, w_ref[:, c0:c0 + width])

    swa_ref[:, 0:512] = roped(C_SWA_Q, 512).astype(BF16)
    swa_ref[:, 512:640] = roped(C_SWA_K, 128).astype(BF16)
    swa_ref[:, 640:768] = plain(C_SWA_V, 128).astype(BF16)
    s5_ref[...] = plain(C_S5_U, 512).astype(BF16)
    ret_ref[:, 0:256] = roped(C_RET_Q, 256).astype(BF16)
    ret_ref[:, 256:512] = roped(C_RET_K, 256).astype(BF16)
    ret_ref[:, 512:1024] = plain(C_RET_V, 512).astype(BF16)
    ret_ref[:, 1024:1536] = plain(C_RET_G, 512).astype(BF16)

    kr = (plain(C_MLA_KR, LANES) * cosr_ref[...]
          + _dot(h, wr_ref[:, C_MLA_KR:C_MLA_KR + LANES]) * sinr_ref[...]).astype(BF16)
    cq = _rms(plain(C_MLA_CQ, MLA_Q_RANK), gq_ref[...]).astype(BF16)
    ckv = _rms(plain(C_MLA_CKV, MLA_KV_RANK), gkv_ref[...]).astype(BF16)
    reps = MLA_HEADS
    q = (_dot(cq, wuq_ref[...]) * jnp.tile(cosq_ref[...], (1, reps))
         + _dot(cq, wuqr_ref[...]) * jnp.tile(sinq_ref[...], (1, reps)))
    mq_ref[...] = q.astype(BF16)
    mk_ref[...] = _dot(jnp.concatenate([ckv, kr], axis=1), wk_ref[...]).astype(BF16)
    mv_ref[...] = _dot(ckv, wv_ref[...]).astype(BF16)


def _inproj(x2, p, tabs, seq, tm):
    t = x2.shape[0]
    nblk = seq // tm
    row = lambda i: (i, 0)
    tab = lambda i: (i % nblk, 0)
    in_specs = [
        pl.BlockSpec((tm, D_MODEL), row),
        _const_spec((1, D_MODEL)),
        _const_spec((D_MODEL, N_IN)),
        _const_spec((D_MODEL, N_ROPED)),
        pl.BlockSpec((tm, LANES), tab), pl.BlockSpec((tm, LANES), tab),
        pl.BlockSpec((tm, LANES), tab), pl.BlockSpec((tm, LANES), tab),
        pl.BlockSpec((tm, LANES), tab), pl.BlockSpec((tm, LANES), tab),
        _const_spec((1, MLA_Q_RANK)),
        _const_spec((1, MLA_KV_RANK)),
        _const_spec((MLA_Q_RANK, MLA_HEADS * MLA_HEAD_PAD)),
        _const_spec((MLA_Q_RANK, MLA_HEADS * MLA_HEAD_PAD)),
        _const_spec((MLA_KV_RANK + LANES, MLA_HEADS * MLA_HEAD_PAD)),
        _const_spec((MLA_KV_RANK, MLA_HEADS * MLA_V_DIM)),
    ]
    widths = (768, 512, 1536, MLA_HEADS * MLA_HEAD_PAD, MLA_HEADS * MLA_HEAD_PAD, MLA_HEADS * MLA_V_DIM)
    return pl.pallas_call(
        _inproj_kernel,
        grid=(t // tm,),
        in_specs=in_specs,
        out_specs=[pl.BlockSpec((tm, w), row) for w in widths],
        out_shape=[jax.ShapeDtypeStruct((t, w), BF16) for w in widths],
        compiler_params=pltpu.CompilerParams(dimension_semantics=("parallel",),
                                             vmem_limit_bytes=VMEM_LIMIT_BYTES),
        name="inproj",
    )(x2, p["g_pre_mix"], p["w1"], p["w1r"], tabs["cos64"], tabs["sin64"], tabs["cosr"], tabs["sinr"],
      tabs["cosq"], tabs["sinq"], p["g_q"], p["g_kv"], p["wuq"], p["wuqr"], p["wk"], p["wv"])


def _swa_kernel(sink_ref, q_ref, kp_ref, kc_ref, vp_ref, vc_ref, o_ref):
    n = pl.program_id(1)
    kk = jnp.concatenate([kp_ref[0], kc_ref[0]], axis=0)
    vv = jnp.concatenate([vp_ref[0], vc_ref[0]], axis=0)
    row = lax.broadcasted_iota(jnp.int32, (BLOCK, 2 * BLOCK), 0)
    col = lax.broadcasted_iota(jnp.int32, (BLOCK, 2 * BLOCK), 1)
    diff = row + BLOCK - col
    first_key = jnp.where(n > 0, 0, BLOCK)
    ok = (diff >= 0) & (diff < BLOCK) & (col >= first_key)
    lane = lax.broadcasted_iota(jnp.int32, (BLOCK, LANES), 1)
    lo = lane < HEAD_DIM
    for j in range(SWA_GROUP):
        q2 = q_ref[0, :, j * LANES:(j + 1) * LANES]
        outs = []
        for half in range(2):
            qm = jnp.where(lo if half == 0 else jnp.logical_not(lo), q2, jnp.zeros_like(q2))
            s = jnp.where(ok, _dot_nt(qm, kk), NEG_INF)
            sink = sink_ref[j + SWA_GROUP * half]
            m = jnp.maximum(jnp.max(s, axis=-1, keepdims=True), sink)
            pr = jnp.exp(s - m)
            denom = jnp.sum(pr, axis=-1, keepdims=True) + jnp.exp(sink - m)
            outs.append(_dot((pr / denom).astype(BF16), vv))
        o_ref[0, :, j * LANES:(j + 1) * LANES] = jnp.where(lo, outs[0], outs[1]).astype(BF16)


def _swa(swa3, sinks):
    b, s, _ = swa3.shape
    nb = s // BLOCK
    cur = lambda c: (lambda bi, n: (bi, n, c))
    prev = lambda c: (lambda bi, n: (bi, jnp.maximum(n - 1, 0), c))
    return pl.pallas_call(
        _swa_kernel,
        grid=(b, nb),
        in_specs=[
            pl.BlockSpec(memory_space=pltpu.SMEM),
            pl.BlockSpec((1, BLOCK, 512), lambda bi, n: (bi, n, 0)),
            pl.BlockSpec((1, BLOCK, LANES), prev(4)), pl.BlockSpec((1, BLOCK, LANES), cur(4)),
            pl.BlockSpec((1, BLOCK, LANES), prev(5)), pl.BlockSpec((1, BLOCK, LANES), cur(5)),
        ],
        out_specs=pl.BlockSpec((1, BLOCK, 512), lambda bi, n: (bi, n, 0)),
        out_shape=jax.ShapeDtypeStruct((b, s, 512), BF16),
        compiler_params=pltpu.CompilerParams(dimension_semantics=("parallel", "parallel")),
        name="swa",
    )(sinks, swa3, swa3, swa3, swa3, swa3)


S5_LANE_CHUNK = 256


def _s5_kernel(u_ref, bmat_ref, cmat_ref, apr_ref, api_ref, d_ref, wglu_ref, o_ref, bu_ref, xb_ref, carry_ref):
    tm = u_ref.shape[1]
    steps = int(math.log2(tm))

    @pl.when(pl.program_id(1) == 0)
    def _():
        carry_ref[...] = jnp.zeros_like(carry_ref)

    u = u_ref[0]
    bu_ref[...] = _dot(u, bmat_ref[...])
    row = lax.broadcasted_iota(jnp.int32, (tm, S5_LANE_CHUNK), 0)
    for c in range(S5_STATES // S5_LANE_CHUNK):
        lo = c * S5_LANE_CHUNK
        sl_re = slice(lo, lo + S5_LANE_CHUNK)
        sl_im = slice(S5_STATES + lo, S5_STATES + lo + S5_LANE_CHUNK)
        re = bu_ref[:, sl_re]
        im = bu_ref[:, sl_im]
        cr = carry_ref[0:1, sl_re]
        ci = carry_ref[0:1, sl_im]
        ar = apr_ref[0:1, sl_re]
        ai = api_ref[0:1, sl_re]
        first = row == 0
        re = re + jnp.where(first, ar * cr - ai * ci, 0.0)
        im = im + jnp.where(first, ar * ci + ai * cr, 0.0)
        for k in range(steps):
            d = 1 << k
            ar = apr_ref[k:k + 1, sl_re]
            ai = api_ref[k:k + 1, sl_re]
            keep = row >= d
            sr = jnp.where(keep, pltpu.roll(re, d, 0), 0.0)
            si = jnp.where(keep, pltpu.roll(im, d, 0), 0.0)
            re, im = re + (ar * sr - ai * si), im + (ar * si + ai * sr)
        carry_ref[0:1, sl_re] = re[tm - 1:tm, :]
        carry_ref[0:1, sl_im] = im[tm - 1:tm, :]
        xb_ref[:, sl_re] = re.astype(BF16)
        xb_ref[:, sl_im] = im.astype(BF16)
    y = _dot(xb_ref[...], cmat_ref[...]) + d_ref[...] * u.astype(F32)
    z = _gelu_tanh(y).astype(BF16)
    zz = _dot(z, wglu_ref[...])
    o_ref[0] = (zz[:, :S5_CHANNELS] * _sigmoid(zz[:, S5_CHANNELS:])).astype(BF16)


def _s5(u3, p, tm):
    b, s, _ = u3.shape
    steps = int(math.log2(tm))
    return pl.pallas_call(
        _s5_kernel,
        grid=(b, s // tm),
        in_specs=[
            pl.BlockSpec((1, tm, S5_CHANNELS), lambda bi, t: (bi, t, 0)),
            _const_spec((S5_CHANNELS, 2 * S5_STATES)),
            _const_spec((2 * S5_STATES, S5_CHANNELS)),
            _const_spec((steps, S5_STATES)),
            _const_spec((steps, S5_STATES)),
            _const_spec((1, S5_CHANNELS)),
            _const_spec((S5_CHANNELS, 2 * S5_CHANNELS)),
        ],
        out_specs=pl.BlockSpec((1, tm, S5_CHANNELS), lambda bi, t: (bi, t, 0)),
        out_shape=jax.ShapeDtypeStruct((b, s, S5_CHANNELS), BF16),
        scratch_shapes=[pltpu.VMEM((tm, 2 * S5_STATES), F32),
                        pltpu.VMEM((tm, 2 * S5_STATES), BF16),
                        pltpu.VMEM((8, 2 * S5_STATES), F32)],
        compiler_params=pltpu.CompilerParams(dimension_semantics=("parallel", "arbitrary"),
                                             vmem_limit_bytes=VMEM_LIMIT_BYTES),
        name="s5",
    )(u3, p["s5_bmat"], p["s5_cmat"], p["s5_apr"][:steps], p["s5_api"][:steps], p["s5_d"], p["s5_wglu"])


def _ret_kernel(qkvg_q_ref, qkvg_k_ref, v_ref, g_ref, decay_ref, qw_ref, kw_ref, o_ref, state_ref, *, chunk_decay):
    @pl.when(pl.program_id(1) == 0)
    def _():
        state_ref[...] = jnp.zeros_like(state_ref)

    tm = v_ref.shape[1]
    lane = lax.broadcasted_iota(jnp.int32, (RET_CHUNK, LANES), 1)
    lo = lane < RET_QK_DIM
    for c in range(tm // RET_CHUNK):
        rows = slice(c * RET_CHUNK, (c + 1) * RET_CHUNK)
        for hd in range(RET_HEADS):
            pair, half = divmod(hd, 2)
            sel = lo if half == 0 else jnp.logical_not(lo)
            q2 = qkvg_q_ref[0, rows, pair * LANES:(pair + 1) * LANES]
            k2 = qkvg_k_ref[0, rows, pair * LANES:(pair + 1) * LANES]
            qm = jnp.where(sel, q2, jnp.zeros_like(q2))
            km = jnp.where(sel, k2, jnp.zeros_like(k2))
            v = v_ref[0, rows, hd * RET_V_DIM:(hd + 1) * RET_V_DIM]
            st = state_ref[hd]
            s = _dot_nt(qm, km) * decay_ref[hd]
            y = _dot(s.astype(BF16), v) + _dot(qm, st.astype(BF16)) * qw_ref[hd]
            mu = jnp.mean(y, axis=-1, keepdims=True)
            yc = y - mu
            var = jnp.mean(yc * yc, axis=-1, keepdims=True)
            yn = yc * lax.rsqrt(var + EPS)
            g = g_ref[0, rows, hd * RET_V_DIM:(hd + 1) * RET_V_DIM].astype(F32)
            o_ref[0, rows, hd * RET_V_DIM:(hd + 1) * RET_V_DIM] = (g * _sigmoid(g) * yn).astype(BF16)
            kd = (km.astype(F32) * kw_ref[hd]).astype(BF16)
            state_ref[hd] = st * chunk_decay[hd] + _dot_tn(kd, v)


def _ret(ret3, tabs, tm):
    b, s, _ = ret3.shape
    log_gamma = [math.log1p(-(2.0 ** (-5.0 - h))) for h in range(RET_HEADS)]
    chunk_decay = tuple(math.exp(lg * RET_CHUNK) for lg in log_gamma)
    blk = lambda width, c: pl.BlockSpec((1, tm, width), lambda bi, t: (bi, t, c))
    tab_spec = _const_spec((RET_HEADS, RET_CHUNK, LANES))
    return pl.pallas_call(
        functools.partial(_ret_kernel, chunk_decay=chunk_decay),
        grid=(b, s // tm),
        in_specs=[blk(256, 0), blk(256, 1), blk(512, 1), blk(512, 2), tab_spec, tab_spec, tab_spec],
        out_specs=pl.BlockSpec((1, tm, 512), lambda bi, t: (bi, t, 0)),
        out_shape=jax.ShapeDtypeStruct((b, s, 512), BF16),
        scratch_shapes=[pltpu.VMEM((RET_HEADS, LANES, RET_V_DIM), F32)],
        compiler_params=pltpu.CompilerParams(dimension_semantics=("parallel", "arbitrary")),
        name="retention",
    )(ret3, ret3, ret3, ret3, tabs["ret_decay"], tabs["ret_qw"], tabs["ret_kw"])


def _mla_kernel(q_ref, k_ref, v_ref, o_ref):
    tq = q_ref.shape[1]
    i = pl.program_id(2)
    lane = lax.broadcasted_iota(jnp.int32, (tq, LANES), 1)
    row = lax.broadcasted_iota(jnp.int32, (tq, tq), 0)
    col = lax.broadcasted_iota(jnp.int32, (tq, tq), 1)
    causal = col <= row

    def tile(j, carry, masked):
        start = pl.multiple_of(j * tq, tq)
        v = v_ref[0, pl.ds(start, tq), :]
        new = []
        for half in range(2):
            m, l, acc = carry[half]
            q = q_ref[0, :, half * MLA_HEAD_PAD:(half + 1) * MLA_HEAD_PAD]
            k = k_ref[0, pl.ds(start, tq), half * MLA_HEAD_PAD:(half + 1) * MLA_HEAD_PAD]
            s = _dot_nt(q, k)
            if masked:
                s = jnp.where(causal, s, NEG_INF)
            m_new = jnp.maximum(m, jnp.max(s, axis=-1, keepdims=True))
            alpha = jnp.exp(m - m_new)
            pr = jnp.exp(s - m_new)
            l = alpha * l + jnp.sum(pr, axis=-1, keepdims=True)
            acc = alpha * acc + _dot(pr.astype(BF16), v)
            new.append((m_new, l, acc))
        return tuple(new)

    init = tuple((jnp.full((tq, 1), NEG_INF, F32), jnp.zeros((tq, 1), F32), jnp.zeros((tq, LANES), F32))
                 for _ in range(2))
    carry = lax.fori_loop(0, i, lambda j, c: tile(j, c, False), init)
    carry = tile(i, carry, True)
    o0 = carry[0][2] / carry[0][1]
    o1 = carry[1][2] / carry[1][1]
    o_ref[0] = jnp.where(lane < MLA_V_DIM, o0, o1).astype(BF16)


def _mla(q3, k3, v3, tq):
    b, s, _ = q3.shape
    pairs = MLA_HEADS // 2
    return pl.pallas_call(
        _mla_kernel,
        grid=(b, pairs, s // tq),
        in_specs=[
            pl.BlockSpec((1, tq, 2 * MLA_HEAD_PAD), lambda bi, hp, i: (bi, i, hp)),
            pl.BlockSpec((1, s, 2 * MLA_HEAD_PAD), lambda bi, hp, i: (bi, 0, hp)),
            pl.BlockSpec((1, s, 2 * MLA_V_DIM), lambda bi, hp, i: (bi, 0, hp)),
        ],
        out_specs=pl.BlockSpec((1, tq, 2 * MLA_V_DIM), lambda bi, hp, i: (bi, i, hp)),
        out_shape=jax.ShapeDtypeStruct((b, s, MLA_HEADS * MLA_V_DIM), BF16),
        compiler_params=pltpu.CompilerParams(dimension_semantics=("parallel", "parallel", "arbitrary")),
        name="mla",
    )(q3, k3, v3)


def _merge_kernel(x_ref, ya_ref, yb_ref, yc_ref, yd_ref, gpre_ref, wg_ref, wb_ref, wo_ref, gpost_ref, o_ref):
    x = x_ref[...]
    h = _rms(x, gpre_ref[...]).astype(BF16)
    merged = None
    for br, y_ref in enumerate((ya_ref, yb_ref, yc_ref, yd_ref)):
        gate = _sigmoid(_dot(h, wg_ref[:, br * D_MODEL:(br + 1) * D_MODEL]))
        term = gate * _dot(y_ref[...], wb_ref[br])
        merged = term if merged is None else merged + term
    o_ref[...] = x + _rms(_dot(merged.astype(BF16), wo_ref[...]), gpost_ref[...])


def _merge(x2, ys, p, tm):
    t = x2.shape[0]
    row = lambda i: (i, 0)
    return pl.pallas_call(
        _merge_kernel,
        grid=(t // tm,),
        in_specs=[pl.BlockSpec((tm, D_MODEL), row)]
        + [pl.BlockSpec((tm, BRANCH_WIDTH), row)] * N_BRANCH
        + [_const_spec((1, D_MODEL)),
           _const_spec((D_MODEL, N_BRANCH * D_MODEL)),
           _const_spec((N_BRANCH, BRANCH_WIDTH, D_MODEL)),
           _const_spec((D_MODEL, D_MODEL)),
           _const_spec((1, D_MODEL))],
        out_specs=pl.BlockSpec((tm, D_MODEL), row),
        out_shape=jax.ShapeDtypeStruct((t, D_MODEL), F32),
        compiler_params=pltpu.CompilerParams(dimension_semantics=("parallel",),
                                             vmem_limit_bytes=VMEM_LIMIT_BYTES),
        name="merge",
    )(x2, *ys, p["g_pre_mix"], p["w_gate"], p["w_branch"], p["w_out"], p["g_post_mix"])


FF_CHUNK = 1024


def _mlp_kernel(x_ref, gpre_ref, w1_ref, w2_ref, gpost_ref, o_ref):
    x = x_ref[...]
    h = _rms(x, gpre_ref[...]).astype(BF16)
    f = None
    for c in range(D_FF // FF_CHUNK):
        a = jnp.maximum(_dot(h, w1_ref[:, c * FF_CHUNK:(c + 1) * FF_CHUNK]), 0.0)
        part = _dot((a * a).astype(BF16), w2_ref[c * FF_CHUNK:(c + 1) * FF_CHUNK, :])
        f = part if f is None else f + part
    o_ref[...] = x + _rms(f, gpost_ref[...])


def _mlp(x2, p, tm):
    t = x2.shape[0]
    row = lambda i: (i, 0)
    return pl.pallas_call(
        _mlp_kernel,
        grid=(t // tm,),
        in_specs=[pl.BlockSpec((tm, D_MODEL), row),
                  _const_spec((1, D_MODEL)),
                  _const_spec((D_MODEL, D_FF)),
                  _const_spec((D_FF, D_MODEL)),
                  _const_spec((1, D_MODEL))],
        out_specs=pl.BlockSpec((tm, D_MODEL), row),
        out_shape=jax.ShapeDtypeStruct((t, D_MODEL), F32),
        compiler_params=pltpu.CompilerParams(dimension_semantics=("parallel",),
                                             vmem_limit_bytes=VMEM_LIMIT_BYTES),
        name="mlp",
    )(x2, p["g_pre_mlp"], p["w_ff1"], p["w_ff2"], p["g_post_mlp"])


def _rot_cols(w, dim):
    k, n = w.shape
    wh = w.reshape(k, n // dim, 2, dim // 2)
    return jnp.concatenate([-wh[:, :, 1], wh[:, :, 0]], axis=-1).reshape(k, n)


def _rope_tables(seq):
    def tab(dim):
        inv = 1.0 / (ROPE_THETA ** (jnp.arange(0, dim, 2, dtype=F32) / dim))
        ang = jnp.arange(seq, dtype=F32)[:, None] * inv[None, :]
        return jnp.cos(ang), jnp.sin(ang)

    c64, s64 = tab(HEAD_DIM)
    c32, s32 = tab(MLA_ROPE_DIM)
    zeros = lambda w: jnp.zeros((seq, w), F32)
    ones = lambda w: jnp.ones((seq, w), F32)
    scale = (MLA_NOPE_DIM + MLA_ROPE_DIM) ** -0.5
    pad = MLA_HEAD_PAD - MLA_NOPE_DIM - MLA_ROPE_DIM
    return {
        "cos64": jnp.tile(c64, (1, 4)), "sin64": jnp.tile(s64, (1, 4)),
        "cosr": jnp.concatenate([c32, c32, zeros(LANES - MLA_ROPE_DIM)], axis=1),
        "sinr": jnp.concatenate([s32, s32, zeros(LANES - MLA_ROPE_DIM)], axis=1),
        "cosq": scale * jnp.concatenate([ones(MLA_NOPE_DIM), c32, c32, zeros(pad)], axis=1),
        "sinq": scale * jnp.concatenate([zeros(MLA_NOPE_DIM), s32, s32, zeros(pad)], axis=1),
    }


def _ret_tables():
    h = RET_HEADS
    c = RET_CHUNK
    log_gamma = jnp.log1p(-jnp.exp2(-5.0 - jnp.arange(h, dtype=F32)))
    idx = jnp.arange(c, dtype=F32)
    diff = idx[:, None] - idx[None, :]
    decay = jnp.where(diff >= 0, jnp.exp(log_gamma[:, None, None] * jnp.maximum(diff, 0.0)), 0.0)
    k_w = jnp.exp(log_gamma[:, None] * (c - 1 - idx)[None, :])
    q_w = jnp.exp(log_gamma[:, None] * (idx + 1.0)[None, :])
    bc = lambda a: jnp.broadcast_to(a[:, :, None], (h, c, LANES))
    return {"ret_decay": decay, "ret_qw": bc(q_w), "ret_kw": bc(k_w)}


def _s5_params(lam_re, lam_im, log_dt, b_re, b_im, c_re, c_im, max_steps):
    g = S5_GROUPS
    dt = jnp.exp(log_dt)[:, None]
    mag = jnp.exp(lam_re * dt)
    ab_re, ab_im = mag * jnp.cos(lam_im * dt), mag * jnp.sin(lam_im * dt)
    den = lam_re * lam_re + lam_im * lam_im
    nr, ni = ab_re - 1.0, ab_im
    f_re = (nr * lam_re + ni * lam_im) / den
    f_im = (ni * lam_re - nr * lam_im) / den
    bb_re = f_re[..., None] * b_re - f_im[..., None] * b_im
    bb_im = f_re[..., None] * b_im + f_im[..., None] * b_re
    eye = jnp.eye(g, dtype=F32)
    in_mat = lambda bb: jnp.einsum("gpc,gh->gchp", bb, eye).reshape(S5_CHANNELS, S5_STATES)
    out_mat = lambda cc: jnp.einsum("gcp,gh->hpgc", cc, eye).reshape(S5_STATES, S5_CHANNELS)
    bmat = jnp.concatenate([in_mat(bb_re), in_mat(bb_im)], axis=1).astype(BF16)
    cmat = jnp.concatenate([out_mat(c_re), -out_mat(c_im)], axis=0).astype(BF16)
    pr, pi = [ab_re.reshape(1, S5_STATES)], [ab_im.reshape(1, S5_STATES)]
    for _ in range(max_steps - 1):
        r, i = pr[-1], pi[-1]
        pr.append(r * r - i * i)
        pi.append(2.0 * r * i)
    return bmat, cmat, jnp.concatenate(pr, axis=0), jnp.concatenate(pi, axis=0)


def _prep_layer(l, a, max_steps):
    w_in = a["w_in"][l]
    offs = np.cumsum([0, 512, 128, 128, 512, 256, 256, 512, 512, 256, 128, 32, 4096])
    (w_sq, w_sk, w_sv, w_su, w_rq, w_rk, w_rv, w_rg, w_cq, w_ckv, w_kr, w_gate) = [
        w_in[:, int(offs[i]):int(offs[i + 1])] for i in range(12)]
    head_order = [h for j in range(SWA_GROUP) for h in (j, SWA_GROUP + j)]
    w_sq = w_sq.reshape(D_MODEL, SWA_HEADS, HEAD_DIM)[:, head_order].reshape(D_MODEL, 512) * (HEAD_DIM ** -0.5)
    w_rk = w_rk * (RET_QK_DIM ** -0.5)
    w_kr_pad = jnp.pad(w_kr, ((0, 0), (0, LANES - MLA_ROPE_DIM)))
    w1 = jnp.concatenate([w_sq, w_sk, w_rq, w_rk, w_kr_pad, w_sv, w_su, w_rv, w_rg, w_cq, w_ckv], axis=1)
    w_kr_rot = jnp.pad(_rot_cols(w_kr, MLA_ROPE_DIM), ((0, 0), (0, LANES - MLA_ROPE_DIM)))
    w1r = jnp.concatenate([_rot_cols(w1[:, :N_ROPED_64], HEAD_DIM), w_kr_rot], axis=1)

    qdim = MLA_NOPE_DIM + MLA_ROPE_DIM
    pad = MLA_HEAD_PAD - qdim
    wuq = a["mla_w_uq"][l].reshape(MLA_Q_RANK, MLA_HEADS, qdim)
    wuq_pad = jnp.pad(wuq, ((0, 0), (0, 0), (0, pad))).reshape(MLA_Q_RANK, MLA_HEADS * MLA_HEAD_PAD)
    rope_rot = _rot_cols(wuq[:, :, MLA_NOPE_DIM:].reshape(MLA_Q_RANK, MLA_HEADS * MLA_ROPE_DIM), MLA_ROPE_DIM)
    wuq_rot = jnp.pad(rope_rot.reshape(MLA_Q_RANK, MLA_HEADS, MLA_ROPE_DIM),
                      ((0, 0), (0, 0), (MLA_NOPE_DIM, pad))).reshape(MLA_Q_RANK, MLA_HEADS * MLA_HEAD_PAD)
    wukv = a["mla_w_ukv"][l].reshape(MLA_KV_RANK, MLA_HEADS, MLA_NOPE_DIM + MLA_V_DIM)
    wk_nope = jnp.pad(wukv[:, :, :MLA_NOPE_DIM], ((0, 0), (0, 0), (0, MLA_HEAD_PAD - MLA_NOPE_DIM)))
    place = jnp.zeros((LANES, MLA_HEADS, MLA_HEAD_PAD), F32).at[
        jnp.arange(MLA_ROPE_DIM), :, MLA_NOPE_DIM + jnp.arange(MLA_ROPE_DIM)].set(1.0)
    wk = jnp.concatenate([wk_nope, place], axis=0).reshape(MLA_KV_RANK + LANES, MLA_HEADS * MLA_HEAD_PAD)
    wv = wukv[:, :, MLA_NOPE_DIM:].reshape(MLA_KV_RANK, MLA_HEADS * MLA_V_DIM)

    bmat, cmat, apr, api = _s5_params(a["s5_lam_re"][l], a["s5_lam_im"][l], a["s5_log_dt"][l],
                                      a["s5_b_re"][l], a["s5_b_im"][l], a["s5_c_re"][l], a["s5_c_im"][l], max_steps)
    wb = a["w_branch"][l]
    wb0 = wb[0].reshape(SWA_HEADS, HEAD_DIM, D_MODEL)[jnp.array(head_order)].reshape(BRANCH_WIDTH, D_MODEL)
    row = lambda v: v.reshape(1, -1).astype(F32)
    return {
        "g_pre_mix": row(a["g_pre_mix"][l]), "g_post_mix": row(a["g_post_mix"][l]),
        "g_pre_mlp": row(a["g_pre_mlp"][l]), "g_post_mlp": row(a["g_post_mlp"][l]),
        "w1": w1.astype(BF16), "w1r": w1r.astype(BF16),
        "g_q": row(a["mla_g_q"][l]), "g_kv": row(a["mla_g_kv"][l]),
        "wuq": wuq_pad.astype(BF16), "wuqr": wuq_rot.astype(BF16), "wk": wk.astype(BF16), "wv": wv.astype(BF16),
        "sinks": a["swa_sinks"][l].astype(F32),
        "s5_bmat": bmat, "s5_cmat": cmat, "s5_apr": apr, "s5_api": api,
        "s5_d": row(a["s5_d"][l]), "s5_wglu": a["s5_w_glu"][l].astype(BF16),
        "w_gate": w_gate.astype(BF16),
        "w_branch": jnp.concatenate([wb0[None], wb[1:]], axis=0).astype(BF16),
        "w_out": a["w_out"][l].astype(BF16),
        "w_ff1": a["w_ff1"][l].astype(BF16), "w_ff2": a["w_ff2"][l].astype(BF16),
    }


def _tile(n, pref):
    t = min(n, pref)
    assert n % t == 0, (n, t)
    return t


def kernel(x, g_pre_mix, g_post_mix, g_pre_mlp, g_post_mlp, w_in, swa_sinks, s5_lam_re, s5_lam_im, s5_log_dt,
           s5_b_re, s5_b_im, s5_c_re, s5_c_im, s5_d, s5_w_glu, mla_g_q, mla_g_kv, mla_w_uq, mla_w_ukv,
           w_branch, w_out, w_ff1, w_ff2):
    args = dict(g_pre_mix=g_pre_mix, g_post_mix=g_post_mix, g_pre_mlp=g_pre_mlp, g_post_mlp=g_post_mlp,
                w_in=w_in, swa_sinks=swa_sinks, s5_lam_re=s5_lam_re, s5_lam_im=s5_lam_im, s5_log_dt=s5_log_dt,
                s5_b_re=s5_b_re, s5_b_im=s5_b_im, s5_c_re=s5_c_re, s5_c_im=s5_c_im, s5_d=s5_d, s5_w_glu=s5_w_glu,
                mla_g_q=mla_g_q, mla_g_kv=mla_g_kv, mla_w_uq=mla_w_uq, mla_w_ukv=mla_w_ukv,
                w_branch=w_branch, w_out=w_out, w_ff1=w_ff1, w_ff2=w_ff2)
    b, s, d = x.shape
    assert d == D_MODEL and s % BLOCK == 0
    t = b * s
    tm_proj = _tile(s, 512)
    tm_s5 = _tile(s, 128)
    tm_ret = _tile(s, 512)
    tq_mla = _tile(s, 256)
    tm_tok = _tile(t, 256)
    tabs = {**_rope_tables(s), **_ret_tables()}
    x2 = x.reshape(t, d)
    for l in range(DEPTH):
        p = _prep_layer(l, args, int(math.log2(tm_s5)))
        swa, s5u, ret, mq, mk, mv = _inproj(x2, p, tabs, s, tm_proj)
        to3 = lambda v: v.reshape(b, s, v.shape[-1])
        y_a = _swa(to3(swa), p["sinks"])
        y_b = _s5(to3(s5u), p, tm_s5)
        y_c = _ret(to3(ret), tabs, tm_ret)
        y_d = _mla(to3(mq), to3(mk), to3(mv), tq_mla)
        ys = [v.reshape(t, BRANCH_WIDTH) for v in (y_a, y_b, y_c, y_d)]
        x2 = _merge(x2, ys, p, tm_tok)
        x2 = _mlp(x2, p, tm_tok)
    return x2.reshape(b, s, d)
```

```python
import functools
import math

import jax
import jax.numpy as jnp
import numpy as np
from jax import lax
from jax.experimental import pallas as pl
from jax.experimental.pallas import tpu as pltpu

F32 = jnp.float32
BF16 = jnp.bfloat16

D_MODEL = 1024
DEPTH = 2
EPS = 1e-6
NEG_INF = -1e30
ROPE_THETA = 10000.0
BLOCK = 128
HEAD_DIM = 64
SWA_HEADS = 8
SWA_KV_HEADS = 2
SWA_GROUP = SWA_HEADS // SWA_KV_HEADS
S5_CHANNELS = 512
S5_GROUP = 16
S5_GROUPS = S5_CHANNELS // S5_GROUP
S5_STATE = 64
S5_STATES = S5_GROUPS * S5_STATE
RET_HEADS = 4
RET_QK_DIM = 64
RET_V_DIM = 128
RET_CHUNK = 128
MLA_HEADS = 8
MLA_Q_RANK = 256
MLA_KV_RANK = 128
MLA_NOPE_DIM = 64
MLA_ROPE_DIM = 32
MLA_V_DIM = 64
MLA_HEAD_PAD = 128
N_BRANCH = 4
BRANCH_WIDTH = 512
D_FF = 4 * D_MODEL

LANES = 128
VMEM_LIMIT_BYTES = 56 * 1024 * 1024

C_SWA_Q = 0
C_SWA_K = 512
C_RET_Q = 640
C_RET_K = 896
C_MLA_KR = 1152
N_ROPED = 1280
C_SWA_V = 1280
C_S5_U = 1408
C_RET_V = 1920
C_RET_G = 2432
C_MLA_CQ = 2944
C_MLA_CKV = 3200
N_IN = 3328
N_ROPED_64 = C_MLA_KR


def _dot(a, b):
    return jnp.dot(a, b, preferred_element_type=F32)


def _dot_nt(a, b):
    return lax.dot_general(a, b, (((1,), (1,)), ((), ())), preferred_element_type=F32)


def _dot_tn(a, b):
    return lax.dot_general(a, b, (((0,), (0,)), ((), ())), preferred_element_type=F32)


def _rms(x, g):
    return x * lax.rsqrt(jnp.mean(x * x, axis=-1, keepdims=True) + EPS) * g


def _sigmoid(x):
    return 1.0 / (1.0 + jnp.exp(-x))


def _gelu_tanh(x):
    return x * (0.5 * (1.0 + jnp.tanh(math.sqrt(2.0 / math.pi) * (x + 0.044715 * (x * x * x)))))


def _const_spec(shape):
    zeros = (0,) * len(shape)
    return pl.BlockSpec(shape, lambda *_: zeros)


def _inproj_kernel(x_ref, g_ref, w_ref, wr_ref, cos_ref, sin_ref, cosr_ref, sinr_ref, cosq_ref, sinq_ref,
                   gq_ref, gkv_ref, wuq_ref, wuqr_ref, wk_ref, wv_ref,
                   swa_ref, s5_ref, ret_ref, mq_ref, mk_ref, mv_ref):
    x = x_ref[...]
    h = _rms(x, g_ref[...]).astype(BF16)
    cos = cos_ref[...]
    sin = sin_ref[...]

    def roped(c0, width):
        z = _dot(h, w_ref[:, c0:c0 + width])
        zr = _dot(h, wr_ref[:, c0:c0 + width])
        reps = width // LANES
        return z * jnp.tile(cos, (1, reps)) + zr * jnp.tile(sin, (1, reps))

    def plain(c0, width):
        return _dot(h, w_ref[:, c0:c0 + width])

    swa_ref[:, 0:512] = roped(C_SWA_Q, 512).astype(BF16)
    swa_ref[:, 512:640] = roped(C_SWA_K, 128).astype(BF16)
    swa_ref[:, 640:768] = plain(C_SWA_V, 128).astype(BF16)
    s5_ref[...] = plain(C_S5_U, 512).astype(BF16)
    ret_ref[:, 0:256] = roped(C_RET_Q, 256).astype(BF16)
    ret_ref[:, 256:512] = roped(C_RET_K, 256).astype(BF16)
    ret_ref[:, 512:1024] = plain(C_RET_V, 512).astype(BF16)
    ret_ref[:, 1024:1536] = plain(C_RET_G, 512).astype(BF16)

    kr = (plain(C_MLA_KR, LANES) * cosr_ref[...]
          + _dot(h, wr_ref[:, C_MLA_KR:C_MLA_KR + LANES]) * sinr_ref[...]).astype(BF16)
    cq = _rms(plain(C_MLA_CQ, MLA_Q_RANK), gq_ref[...]).astype(BF16)
    ckv = _rms(plain(C_MLA_CKV, MLA_KV_RANK), gkv_ref[...]).astype(BF16)
    reps = MLA_HEADS
    q = (_dot(cq, wuq_ref[...]) * jnp.tile(cosq_ref[...], (1, reps))
         + _dot(cq, wuqr_ref[...]) * jnp.tile(sinq_ref[...], (1, reps)))
    mq_ref[...] = q.astype(BF16)
    mk_ref[...] = _dot(jnp.concatenate([ckv, kr], axis=1), wk_ref[...]).astype(BF16)
    mv_ref[...] = _dot(ckv, wv_ref[...]).astype(BF16)


def _inproj(x2, p, tabs, seq, tm):
    t = x2.shape[0]
    nblk = seq // tm
    row = lambda i: (i, 0)
    tab = lambda i: (i % nblk, 0)
    in_specs = [
        pl.BlockSpec((tm, D_MODEL), row),
        _const_spec((1, D_MODEL)),
        _const_spec((D_MODEL, N_IN)),
        _const_spec((D_MODEL, N_ROPED)),
        pl.BlockSpec((tm, LANES), tab), pl.BlockSpec((tm, LANES), tab),
        pl.BlockSpec((tm, LANES), tab), pl.BlockSpec((tm, LANES), tab),
        pl.BlockSpec((tm, LANES), tab), pl.BlockSpec((tm, LANES), tab),
        _const_spec((1, MLA_Q_RANK)),
        _const_spec((1, MLA_KV_RANK)),
        _const_spec((MLA_Q_RANK, MLA_HEADS * MLA_HEAD_PAD)),
        _const_spec((MLA_Q_RANK, MLA_HEADS * MLA_HEAD_PAD)),
        _const_spec((MLA_KV_RANK + LANES, MLA_HEADS * MLA_HEAD_PAD)),
        _const_spec((MLA_KV_RANK, MLA_HEADS * MLA_V_DIM)),
    ]
    widths = (768, 512, 1536, MLA_HEADS * MLA_HEAD_PAD, MLA_HEADS * MLA_HEAD_PAD, MLA_HEADS * MLA_V_DIM)
    return pl.pallas_call(
        _inproj_kernel,
        grid=(t // tm,),
        in_specs=in_specs,
        out_specs=[pl.BlockSpec((tm, w), row) for w in widths],
        out_shape=[jax.ShapeDtypeStruct((t, w), BF16) for w in widths],
        compiler_params=pltpu.CompilerParams(dimension_semantics=("parallel",),
                                             vmem_limit_bytes=VMEM_LIMIT_BYTES),
        name="inproj",
    )(x2, p["g_pre_mix"], p["w1"], p["w1r"], tabs["cos64"], tabs["sin64"], tabs["cosr"], tabs["sinr"],
      tabs["cosq"], tabs["sinq"], p["g_q"], p["g_kv"], p["wuq"], p["wuqr"], p["wk"], p["wv"])


def _swa_kernel(sink_ref, q_ref, kp_ref, kc_ref, vp_ref, vc_ref, o_ref):
    n = pl.program_id(1)
    kk = jnp.concatenate([kp_ref[0], kc_ref[0]], axis=0)
    vv = jnp.concatenate([vp_ref[0], vc_ref[0]], axis=0)
    row = lax.broadcasted_iota(jnp.int32, (BLOCK, 2 * BLOCK), 0)
    col = lax.broadcasted_iota(jnp.int32, (BLOCK, 2 * BLOCK), 1)
    diff = row + BLOCK - col
    first_key = jnp.where(n > 0, 0, BLOCK)
    ok = (diff >= 0) & (diff < BLOCK) & (col >= first_key)
    lane = lax.broadcasted_iota(jnp.int32, (BLOCK, LANES), 1)
    lo = lane < HEAD_DIM
    for j in range(SWA_GROUP):
        q2 = q_ref[0, :, j * LANES:(j + 1) * LANES]
        outs = []
        for half in range(2):
            qm = jnp.where(lo if half == 0 else jnp.logical_not(lo), q2, jnp.zeros_like(q2))
            s = jnp.where(ok, _dot_nt(qm, kk), NEG_INF)
            sink = sink_ref[j + SWA_GROUP * half]
            m = jnp.maximum(jnp.max(s, axis=-1, keepdims=True), sink)
            pr = jnp.exp(s - m)
            denom = jnp.sum(pr, axis=-1, keepdims=True) + jnp.exp(sink - m)
            outs.append(_dot((pr / denom).astype(BF16), vv))
        o_ref[0, :, j * LANES:(j + 1) * LANES] = jnp.where(lo, outs[0], outs[1]).astype(BF16)


def _swa(swa3, sinks):
    b, s, _ = swa3.shape
    nb = s // BLOCK
    cur = lambda c: (lambda bi, n: (bi, n, c))
    prev = lambda c: (lambda bi, n: (bi, jnp.maximum(n - 1, 0), c))
    return pl.pallas_call(
        _swa_kernel,
        grid=(b, nb),
        in_specs=[
            pl.BlockSpec(memory_space=pltpu.SMEM),
            pl.BlockSpec((1, BLOCK, 512), lambda bi, n: (bi, n, 0)),
            pl.BlockSpec((1, BLOCK, LANES), prev(4)), pl.BlockSpec((1, BLOCK, LANES), cur(4)),
            pl.BlockSpec((1, BLOCK, LANES), prev(5)), pl.BlockSpec((1, BLOCK, LANES), cur(5)),
        ],
        out_specs=pl.BlockSpec((1, BLOCK, 512), lambda bi, n: (bi, n, 0)),
        out_shape=jax.ShapeDtypeStruct((b, s, 512), BF16),
        compiler_params=pltpu.CompilerParams(dimension_semantics=("parallel", "parallel")),
        name="swa",
    )(sinks, swa3, swa3, swa3, swa3, swa3)


S5_LANE_CHUNK = 256


def _s5_kernel(u_ref, bmat_ref, cmat_ref, apr_ref, api_ref, d_ref, wglu_ref, o_ref, bu_ref, xb_ref, carry_ref):
    tm = u_ref.shape[1]
    steps = int(math.log2(tm))

    @pl.when(pl.program_id(1) == 0)
    def _():
        carry_ref[...] = jnp.zeros_like(carry_ref)

    u = u_ref[0]
    bu_ref[...] = _dot(u, bmat_ref[...])
    row = lax.broadcasted_iota(jnp.int32, (tm, S5_LANE_CHUNK), 0)
    for c in range(S5_STATES // S5_LANE_CHUNK):
        lo = c * S5_LANE_CHUNK
        sl_re = slice(lo, lo + S5_LANE_CHUNK)
        sl_im = slice(S5_STATES + lo, S5_STATES + lo + S5_LANE_CHUNK)
        re = bu_ref[:, sl_re]
        im = bu_ref[:, sl_im]
        cr = carry_ref[0:1, sl_re]
        ci = carry_ref[0:1, sl_im]
        ar = apr_ref[0:1, sl_re]
        ai = api_ref[0:1, sl_re]
        first = row == 0
        re = re + jnp.where(first, ar * cr - ai * ci, 0.0)
        im = im + jnp.where(first, ar * ci + ai * cr, 0.0)
        for k in range(steps):
            d = 1 << k
            ar = apr_ref[k:k + 1, sl_re]
            ai = api_ref[k:k + 1, sl_re]
            keep = row >= d
            sr = jnp.where(keep, pltpu.roll(re, d, 0), 0.0)
            si = jnp.where(keep, pltpu.roll(im, d, 0), 0.0)
            re, im = re + (ar * sr - ai * si), im + (ar * si + ai * sr)
        carry_ref[0:1, sl_re] = re[tm - 1:tm, :]
        carry_ref[0:1, sl_im] = im[tm - 1:tm, :]
        xb_ref[:, sl_re] = re.astype(BF16)
        xb_ref[:, sl_im] = im.astype(BF16)
    y = _dot(xb_ref[...], cmat_ref[...]) + d_ref[...] * u.astype(F32)
    z = _gelu_tanh(y).astype(BF16)
    zz = _dot(z, wglu_ref[...])
    o_ref[0] = (zz[:, :S5_CHANNELS] * _sigmoid(zz[:, S5_CHANNELS:])).astype(BF16)


def _s5(u3, p, tm):
    b, s, _ = u3.shape
    steps = int(math.log2(tm))
    return pl.pallas_call(
        _s5_kernel,
        grid=(b, s // tm),
        in_specs=[
            pl.BlockSpec((1, tm, S5_CHANNELS), lambda bi, t: (bi, t, 0)),
            _const_spec((S5_CHANNELS, 2 * S5_STATES)),
            _const_spec((2 * S5_STATES, S5_CHANNELS)),
            _const_spec((steps, S5_STATES)),
            _const_spec((steps, S5_STATES)),
            _const_spec((1, S5_CHANNELS)),
            _const_spec((S5_CHANNELS, 2 * S5_CHANNELS)),
        ],
        out_specs=pl.BlockSpec((1, tm, S5_CHANNELS), lambda bi, t: (bi, t, 0)),
        out_shape=jax.ShapeDtypeStruct((b, s, S5_CHANNELS), BF16),
        scratch_shapes=[pltpu.VMEM((tm, 2 * S5_STATES), F32),
                        pltpu.VMEM((tm, 2 * S5_STATES), BF16),
                        pltpu.VMEM((8, 2 * S5_STATES), F32)],
        compiler_params=pltpu.CompilerParams(dimension_semantics=("parallel", "arbitrary"),
                                             vmem_limit_bytes=VMEM_LIMIT_BYTES),
        name="s5",
    )(u3, p["s5_bmat"], p["s5_cmat"], p["s5_apr"][:steps], p["s5_api"][:steps], p["s5_d"], p["s5_wglu"])


def _ret_kernel(qkvg_q_ref, qkvg_k_ref, v_ref, g_ref, decay_ref, qw_ref, kw_ref, o_ref, state_ref, *, chunk_decay):
    @pl.when(pl.program_id(1) == 0)
    def _():
        state_ref[...] = jnp.zeros_like(state_ref)

    tm = v_ref.shape[1]
    lane = lax.broadcasted_iota(jnp.int32, (RET_CHUNK, LANES), 1)
    lo = lane < RET_QK_DIM
    for c in range(tm // RET_CHUNK):
        rows = slice(c * RET_CHUNK, (c + 1) * RET_CHUNK)
        for hd in range(RET_HEADS):
            pair, half = divmod(hd, 2)
            sel = lo if half == 0 else jnp.logical_not(lo)
            q2 = qkvg_q_ref[0, rows, pair * LANES:(pair + 1) * LANES]
            k2 = qkvg_k_ref[0, rows, pair * LANES:(pair + 1) * LANES]
            qm = jnp.where(sel, q2, jnp.zeros_like(q2))
            km = jnp.where(sel, k2, jnp.zeros_like(k2))
            v = v_ref[0, rows, hd * RET_V_DIM:(hd + 1) * RET_V_DIM]
            st = state_ref[hd]
            s = _dot_nt(qm, km) * decay_ref[hd]
            y = _dot(s.astype(BF16), v) + _dot(qm, st.astype(BF16)) * qw_ref[hd]
            mu = jnp.mean(y, axis=-1, keepdims=True)
            yc = y - mu
            var = jnp.mean(yc * yc, axis=-1, keepdims=True)
            yn = yc * lax.rsqrt(var + EPS)
            g = g_ref[0, rows, hd * RET_V_DIM:(hd + 1) * RET_V_DIM].astype(F32)
            o_ref[0, rows, hd * RET_V_DIM:(hd + 1) * RET_V_DIM] = (g * _sigmoid(g) * yn).astype(BF16)
            kd = (km.astype(F32) * kw_ref[hd]).astype(BF16)
            state_ref[hd] = st * chunk_decay[hd] + _dot_tn(kd, v)


def _ret(ret3, tabs, tm):
    b, s, _ = ret3.shape
    log_gamma = [math.log1p(-(2.0 ** (-5.0 - h))) for h in range(RET_HEADS)]
    chunk_decay = tuple(math.exp(lg * RET_CHUNK) for lg in log_gamma)
    blk = lambda width, c: pl.BlockSpec((1, tm, width), lambda bi, t: (bi, t, c))
    tab_spec = _const_spec((RET_HEADS, RET_CHUNK, LANES))
    return pl.pallas_call(
        functools.partial(_ret_kernel, chunk_decay=chunk_decay),
        grid=(b, s // tm),
        in_specs=[blk(256, 0), blk(256, 1), blk(512, 1), blk(512, 2), tab_spec, tab_spec, tab_spec],
        out_specs=pl.BlockSpec((1, tm, 512), lambda bi, t: (bi, t, 0)),
        out_shape=jax.ShapeDtypeStruct((b, s, 512), BF16),
        scratch_shapes=[pltpu.VMEM((RET_HEADS, LANES, RET_V_DIM), F32)],
        compiler_params=pltpu.CompilerParams(dimension_semantics=("parallel", "arbitrary")),
        name="retention",
    )(ret3, ret3, ret3, ret3, tabs["ret_decay"], tabs["ret_qw"], tabs["ret_kw"])


MLA_HEADS_PER_STEP = 4
def _mla_kernel(q_ref, k_ref, v_ref, o_ref, m_ref, l_ref, acc_ref):
    tq = q_ref.shape[1]
    i = pl.program_id(2)
    m_ref[...] = jnp.full(m_ref.shape, NEG_INF, F32)
    l_ref[...] = jnp.zeros(l_ref.shape, F32)
    acc_ref[...] = jnp.zeros(acc_ref.shape, F32)

    def tile(j, masked):
        start = pl.multiple_of(j * tq, tq)
        for hd in range(MLA_HEADS_PER_STEP):
            pair = hd // 2
            v = v_ref[0, pl.ds(start, tq), pair * LANES:(pair + 1) * LANES]
            q = q_ref[0, :, hd * MLA_HEAD_PAD:(hd + 1) * MLA_HEAD_PAD]
            k = k_ref[0, pl.ds(start, tq), hd * MLA_HEAD_PAD:(hd + 1) * MLA_HEAD_PAD]
            s = _dot_nt(q, k)
            if masked:
                row = lax.broadcasted_iota(jnp.int32, (tq, tq), 0)
                col = lax.broadcasted_iota(jnp.int32, (tq, tq), 1)
                s = jnp.where(col <= row, s, NEG_INF)
            blocks = tq // LANES
            m_prev = m_ref[hd]
            s_max = functools.reduce(jnp.maximum, [s[:, c * LANES:(c + 1) * LANES] for c in range(blocks)])
            m_next = jnp.maximum(m_prev, jnp.max(s_max, axis=1)[:, None])
            pr = jnp.exp2(s - jnp.tile(m_next, (1, blocks)))
            alpha = jnp.exp2(m_prev - m_next)
            p_sum = functools.reduce(jnp.add, [pr[:, c * LANES:(c + 1) * LANES] for c in range(blocks)])
            l_ref[hd] = alpha * l_ref[hd] + jnp.sum(p_sum, axis=1)[:, None]
            m_ref[hd] = m_next
            acc_ref[hd] = alpha * acc_ref[hd] + _dot(pr.astype(BF16), v)

    def body(j, carry):
        tile(j, False)
        return carry

    lax.fori_loop(0, i, body, 0)
    tile(i, True)
    lane = lax.broadcasted_iota(jnp.int32, (tq, LANES), 1)
    for pair in range(MLA_HEADS_PER_STEP // 2):
        o0 = acc_ref[2 * pair] / l_ref[2 * pair]
        o1 = acc_ref[2 * pair + 1] / l_ref[2 * pair + 1]
        o_ref[0, :, pair * LANES:(pair + 1) * LANES] = jnp.where(lane < MLA_V_DIM, o0, o1).astype(BF16)


def _mla(q3, k3, v3, tq):
    b, s, _ = q3.shape
    hps = MLA_HEADS_PER_STEP
    stat = pltpu.VMEM((hps, tq, LANES), F32)
    return pl.pallas_call(
        _mla_kernel,
        grid=(b, MLA_HEADS // hps, s // tq),
        in_specs=[
            pl.BlockSpec((1, tq, hps * MLA_HEAD_PAD), lambda bi, hg, i: (bi, i, hg)),
            pl.BlockSpec((1, s, hps * MLA_HEAD_PAD), lambda bi, hg, i: (bi, 0, hg)),
            pl.BlockSpec((1, s, hps * MLA_V_DIM), lambda bi, hg, i: (bi, 0, hg)),
        ],
        out_specs=pl.BlockSpec((1, tq, hps * MLA_V_DIM), lambda bi, hg, i: (bi, i, hg)),
        out_shape=jax.ShapeDtypeStruct((b, s, MLA_HEADS * MLA_V_DIM), BF16),
        scratch_shapes=[stat, stat, stat],
        compiler_params=pltpu.CompilerParams(dimension_semantics=("parallel", "parallel", "arbitrary"),
                                             vmem_limit_bytes=VMEM_LIMIT_BYTES),
        name="mla",
    )(q3, k3, v3)


def _merge_kernel(x_ref, ya_ref, yb_ref, yc_ref, yd_ref, gpre_ref, wg_ref, wb_ref, wo_ref, gpost_ref, o_ref):
    x = x_ref[...]
    h = _rms(x, gpre_ref[...]).astype(BF16)
    merged = None
    for br, y_ref in enumerate((ya_ref, yb_ref, yc_ref, yd_ref)):
        gate = _sigmoid(_dot(h, wg_ref[:, br * D_MODEL:(br + 1) * D_MODEL]))
        term = gate * _dot(y_ref[...], wb_ref[br])
        merged = term if merged is None else merged + term
    o_ref[...] = x + _rms(_dot(merged.astype(BF16), wo_ref[...]), gpost_ref[...])


def _merge(x2, ys, p, tm):
    t = x2.shape[0]
    row = lambda i: (i, 0)
    return pl.pallas_call(
        _merge_kernel,
        grid=(t // tm,),
        in_specs=[pl.BlockSpec((tm, D_MODEL), row)]
        + [pl.BlockSpec((tm, BRANCH_WIDTH), row)] * N_BRANCH
        + [_const_spec((1, D_MODEL)),
           _const_spec((D_MODEL, N_BRANCH * D_MODEL)),
           _const_spec((N_BRANCH, BRANCH_WIDTH, D_MODEL)),
           _const_spec((D_MODEL, D_MODEL)),
           _const_spec((1, D_MODEL))],
        out_specs=pl.BlockSpec((tm, D_MODEL), row),
        out_shape=jax.ShapeDtypeStruct((t, D_MODEL), F32),
        compiler_params=pltpu.CompilerParams(dimension_semantics=("parallel",),
                                             vmem_limit_bytes=VMEM_LIMIT_BYTES),
        name="merge",
    )(x2, *ys, p["g_pre_mix"], p["w_gate"], p["w_branch"], p["w_out"], p["g_post_mix"])


FF_CHUNK = 1024


def _mlp_kernel(x_ref, gpre_ref, w1_ref, w2_ref, gpost_ref, o_ref):
    x = x_ref[...]
    h = _rms(x, gpre_ref[...]).astype(BF16)
    f = None
    for c in range(D_FF // FF_CHUNK):
        a = jnp.maximum(_dot(h, w1_ref[:, c * FF_CHUNK:(c + 1) * FF_CHUNK]), 0.0)
        part = _dot((a * a).astype(BF16), w2_ref[c * FF_CHUNK:(c + 1) * FF_CHUNK, :])
        f = part if f is None else f + part
    o_ref[...] = x + _rms(f, gpost_ref[...])


def _mlp(x2, p, tm):
    t = x2.shape[0]
    row = lambda i: (i, 0)
    return pl.pallas_call(
        _mlp_kernel,
        grid=(t // tm,),
        in_specs=[pl.BlockSpec((tm, D_MODEL), row),
                  _const_spec((1, D_MODEL)),
                  _const_spec((D_MODEL, D_FF)),
                  _const_spec((D_FF, D_MODEL)),
                  _const_spec((1, D_MODEL))],
        out_specs=pl.BlockSpec((tm, D_MODEL), row),
        out_shape=jax.ShapeDtypeStruct((t, D_MODEL), F32),
        compiler_params=pltpu.CompilerParams(dimension_semantics=("parallel",),
                                             vmem_limit_bytes=VMEM_LIMIT_BYTES),
        name="mlp",
    )(x2, p["g_pre_mlp"], p["w_ff1"], p["w_ff2"], p["g_post_mlp"])


def _rot_cols(w, dim):
    k, n = w.shape
    wh = w.reshape(k, n // dim, 2, dim // 2)
    return jnp.concatenate([-wh[:, :, 1], wh[:, :, 0]], axis=-1).reshape(k, n)


def _rope_tables(seq):
    def tab(dim):
        inv = 1.0 / (ROPE_THETA ** (jnp.arange(0, dim, 2, dtype=F32) / dim))
        ang = jnp.arange(seq, dtype=F32)[:, None] * inv[None, :]
        return jnp.cos(ang), jnp.sin(ang)

    c64, s64 = tab(HEAD_DIM)
    c32, s32 = tab(MLA_ROPE_DIM)
    zeros = lambda w: jnp.zeros((seq, w), F32)
    ones = lambda w: jnp.ones((seq, w), F32)
    scale = (MLA_NOPE_DIM + MLA_ROPE_DIM) ** -0.5 * math.log2(math.e)
    pad = MLA_HEAD_PAD - MLA_NOPE_DIM - MLA_ROPE_DIM
    return {
        "cos64": jnp.tile(c64, (1, 4)), "sin64": jnp.tile(s64, (1, 4)),
        "cosr": jnp.concatenate([c32, c32, zeros(LANES - MLA_ROPE_DIM)], axis=1),
        "sinr": jnp.concatenate([s32, s32, zeros(LANES - MLA_ROPE_DIM)], axis=1),
        "cosq": scale * jnp.concatenate([ones(MLA_NOPE_DIM), c32, c32, zeros(pad)], axis=1),
        "sinq": scale * jnp.concatenate([zeros(MLA_NOPE_DIM), s32, s32, zeros(pad)], axis=1),
    }


def _ret_tables():
    h = RET_HEADS
    c = RET_CHUNK
    log_gamma = jnp.log1p(-jnp.exp2(-5.0 - jnp.arange(h, dtype=F32)))
    idx = jnp.arange(c, dtype=F32)
    diff = idx[:, None] - idx[None, :]
    decay = jnp.where(diff >= 0, jnp.exp(log_gamma[:, None, None] * jnp.maximum(diff, 0.0)), 0.0)
    k_w = jnp.exp(log_gamma[:, None] * (c - 1 - idx)[None, :])
    q_w = jnp.exp(log_gamma[:, None] * (idx + 1.0)[None, :])
    bc = lambda a: jnp.broadcast_to(a[:, :, None], (h, c, LANES))
    return {"ret_decay": decay, "ret_qw": bc(q_w), "ret_kw": bc(k_w)}


def _s5_params(lam_re, lam_im, log_dt, b_re, b_im, c_re, c_im, max_steps):
    g = S5_GROUPS
    dt = jnp.exp(log_dt)[:, None]
    mag = jnp.exp(lam_re * dt)
    ab_re, ab_im = mag * jnp.cos(lam_im * dt), mag * jnp.sin(lam_im * dt)
    den = lam_re * lam_re + lam_im * lam_im
    nr, ni = ab_re - 1.0, ab_im
    f_re = (nr * lam_re + ni * lam_im) / den
    f_im = (ni * lam_re - nr * lam_im) / den
    bb_re = f_re[..., None] * b_re - f_im[..., None] * b_im
    bb_im = f_re[..., None] * b_im + f_im[..., None] * b_re
    eye = jnp.eye(g, dtype=F32)
    in_mat = lambda bb: jnp.einsum("gpc,gh->gchp", bb, eye).reshape(S5_CHANNELS, S5_STATES)
    out_mat = lambda cc: jnp.einsum("gcp,gh->hpgc", cc, eye).reshape(S5_STATES, S5_CHANNELS)
    bmat = jnp.concatenate([in_mat(bb_re), in_mat(bb_im)], axis=1).astype(BF16)
    cmat = jnp.concatenate([out_mat(c_re), -out_mat(c_im)], axis=0).astype(BF16)
    pr, pi = [ab_re.reshape(1, S5_STATES)], [ab_im.reshape(1, S5_STATES)]
    for _ in range(max_steps - 1):
        r, i = pr[-1], pi[-1]
        pr.append(r * r - i * i)
        pi.append(2.0 * r * i)
    return bmat, cmat, jnp.concatenate(pr, axis=0), jnp.concatenate(pi, axis=0)


def _prep_layer(l, a, max_steps):
    w_in = a["w_in"][l]
    offs = np.cumsum([0, 512, 128, 128, 512, 256, 256, 512, 512, 256, 128, 32, 4096])
    (w_sq, w_sk, w_sv, w_su, w_rq, w_rk, w_rv, w_rg, w_cq, w_ckv, w_kr, w_gate) = [
        w_in[:, int(offs[i]):int(offs[i + 1])] for i in range(12)]
    head_order = [h for j in range(SWA_GROUP) for h in (j, SWA_GROUP + j)]
    w_sq = w_sq.reshape(D_MODEL, SWA_HEADS, HEAD_DIM)[:, head_order].reshape(D_MODEL, 512) * (HEAD_DIM ** -0.5)
    w_rk = w_rk * (RET_QK_DIM ** -0.5)
    w_kr_pad = jnp.pad(w_kr, ((0, 0), (0, LANES - MLA_ROPE_DIM)))
    w1 = jnp.concatenate([w_sq, w_sk, w_rq, w_rk, w_kr_pad, w_sv, w_su, w_rv, w_rg, w_cq, w_ckv], axis=1)
    w_kr_rot = jnp.pad(_rot_cols(w_kr, MLA_ROPE_DIM), ((0, 0), (0, LANES - MLA_ROPE_DIM)))
    w1r = jnp.concatenate([_rot_cols(w1[:, :N_ROPED_64], HEAD_DIM), w_kr_rot], axis=1)

    qdim = MLA_NOPE_DIM + MLA_ROPE_DIM
    pad = MLA_HEAD_PAD - qdim
    wuq = a["mla_w_uq"][l].reshape(MLA_Q_RANK, MLA_HEADS, qdim)
    wuq_pad = jnp.pad(wuq, ((0, 0), (0, 0), (0, pad))).reshape(MLA_Q_RANK, MLA_HEADS * MLA_HEAD_PAD)
    rope_rot = _rot_cols(wuq[:, :, MLA_NOPE_DIM:].reshape(MLA_Q_RANK, MLA_HEADS * MLA_ROPE_DIM), MLA_ROPE_DIM)
    wuq_rot = jnp.pad(rope_rot.reshape(MLA_Q_RANK, MLA_HEADS, MLA_ROPE_DIM),
                      ((0, 0), (0, 0), (MLA_NOPE_DIM, pad))).reshape(MLA_Q_RANK, MLA_HEADS * MLA_HEAD_PAD)
    wukv = a["mla_w_ukv"][l].reshape(MLA_KV_RANK, MLA_HEADS, MLA_NOPE_DIM + MLA_V_DIM)
    wk_nope = jnp.pad(wukv[:, :, :MLA_NOPE_DIM], ((0, 0), (0, 0), (0, MLA_HEAD_PAD - MLA_NOPE_DIM)))
    place = jnp.zeros((LANES, MLA_HEADS, MLA_HEAD_PAD), F32).at[
        jnp.arange(MLA_ROPE_DIM), :, MLA_NOPE_DIM + jnp.arange(MLA_ROPE_DIM)].set(1.0)
    wk = jnp.concatenate([wk_nope, place], axis=0).reshape(MLA_KV_RANK + LANES, MLA_HEADS * MLA_HEAD_PAD)
    wv = wukv[:, :, MLA_NOPE_DIM:].reshape(MLA_KV_RANK, MLA_HEADS * MLA_V_DIM)

    bmat, cmat, apr, api = _s5_params(a["s5_lam_re"][l], a["s5_lam_im"][l], a["s5_log_dt"][l],
                                      a["s5_b_re"][l], a["s5_b_im"][l], a["s5_c_re"][l], a["s5_c_im"][l], max_steps)
    wb = a["w_branch"][l]
    wb0 = wb[0].reshape(SWA_HEADS, HEAD_DIM, D_MODEL)[jnp.array(head_order)].reshape(BRANCH_WIDTH, D_MODEL)
    row = lambda v: v.reshape(1, -1).astype(F32)
    return {
        "g_pre_mix": row(a["g_pre_mix"][l]), "g_post_mix": row(a["g_post_mix"][l]),
        "g_pre_mlp": row(a["g_pre_mlp"][l]), "g_post_mlp": row(a["g_post_mlp"][l]),
        "w1": w1.astype(BF16), "w1r": w1r.astype(BF16),
        "g_q": row(a["mla_g_q"][l]), "g_kv": row(a["mla_g_kv"][l]),
        "wuq": wuq_pad.astype(BF16), "wuqr": wuq_rot.astype(BF16), "wk": wk.astype(BF16), "wv": wv.astype(BF16),
        "sinks": a["swa_sinks"][l].astype(F32),
        "s5_bmat": bmat, "s5_cmat": cmat, "s5_apr": apr, "s5_api": api,
        "s5_d": row(a["s5_d"][l]), "s5_wglu": a["s5_w_glu"][l].astype(BF16),
        "w_gate": w_gate.astype(BF16),
        "w_branch": jnp.concatenate([wb0[None], wb[1:]], axis=0).astype(BF16),
        "w_out": a["w_out"][l].astype(BF16),
        "w_ff1": a["w_ff1"][l].astype(BF16), "w_ff2": a["w_ff2"][l].astype(BF16),
    }


def _tile(n, pref):
    t = min(n, pref)
    assert n % t == 0, (n, t)
    return t


def kernel(x, g_pre_mix, g_post_mix, g_pre_mlp, g_post_mlp, w_in, swa_sinks, s5_lam_re, s5_lam_im, s5_log_dt,
           s5_b_re, s5_b_im, s5_c_re, s5_c_im, s5_d, s5_w_glu, mla_g_q, mla_g_kv, mla_w_uq, mla_w_ukv,
           w_branch, w_out, w_ff1, w_ff2):
    args = dict(g_pre_mix=g_pre_mix, g_post_mix=g_post_mix, g_pre_mlp=g_pre_mlp, g_post_mlp=g_post_mlp,
                w_in=w_in, swa_sinks=swa_sinks, s5_lam_re=s5_lam_re, s5_lam_im=s5_lam_im, s5_log_dt=s5_log_dt,
                s5_b_re=s5_b_re, s5_b_im=s5_b_im, s5_c_re=s5_c_re, s5_c_im=s5_c_im, s5_d=s5_d, s5_w_glu=s5_w_glu,
                mla_g_q=mla_g_q, mla_g_kv=mla_g_kv, mla_w_uq=mla_w_uq, mla_w_ukv=mla_w_ukv,
                w_branch=w_branch, w_out=w_out, w_ff1=w_ff1, w_ff2=w_ff2)
    b, s, d = x.shape
    assert d == D_MODEL and s % BLOCK == 0
    t = b * s
    tm_proj = _tile(s, 512)
    tm_s5 = _tile(s, 128)
    tm_ret = _tile(s, 512)
    tq_mla = _tile(s, 512)
    tm_tok = _tile(t, 256)
    tabs = {**_rope_tables(s), **_ret_tables()}
    x2 = x.reshape(t, d)
    for l in range(DEPTH):
        p = _prep_layer(l, args, int(math.log2(tm_s5)))
        swa, s5u, ret, mq, mk, mv = _inproj(x2, p, tabs, s, tm_proj)
        to3 = lambda v: v.reshape(b, s, v.shape[-1])
        y_a = _swa(to3(swa), p["sinks"])
        y_b = _s5(to3(s5u), p, tm_s5)
        y_c = _ret(to3(ret), tabs, tm_ret)
        y_d = _mla(to3(mq), to3(mk), to3(mv), tq_mla)
        ys = [v.reshape(t, BRANCH_WIDTH) for v in (y_a, y_b, y_c, y_d)]
        x2 = _merge(x2, ys, p, tm_tok)
        x2 = _mlp(x2, p, tm_tok)
    return x2.reshape(b, s, d)
```

```python
import functools
import math

import jax
import jax.numpy as jnp
import numpy as np
from jax import lax
from jax.experimental import pallas as pl
from jax.experimental.pallas import tpu as pltpu

F32 = jnp.float32
BF16 = jnp.bfloat16

D_MODEL = 1024
DEPTH = 2
EPS = 1e-6
NEG_INF = -1e30
ROPE_THETA = 10000.0
BLOCK = 128
HEAD_DIM = 64
SWA_HEADS = 8
SWA_KV_HEADS = 2
SWA_GROUP = SWA_HEADS // SWA_KV_HEADS
S5_CHANNELS = 512
S5_GROUP = 16
S5_GROUPS = S5_CHANNELS // S5_GROUP
S5_STATE = 64
S5_STATES = S5_GROUPS * S5_STATE
RET_HEADS = 4
RET_QK_DIM = 64
RET_V_DIM = 128
RET_CHUNK = 128
MLA_HEADS = 8
MLA_Q_RANK = 256
MLA_KV_RANK = 128
MLA_NOPE_DIM = 64
MLA_ROPE_DIM = 32
MLA_V_DIM = 64
MLA_HEAD_PAD = 128
N_BRANCH = 4
BRANCH_WIDTH = 512
D_FF = 4 * D_MODEL

LANES = 128
VMEM_LIMIT_BYTES = 56 * 1024 * 1024

C_SWA_Q = 0
C_SWA_K = 512
C_RET_Q = 640
C_RET_K = 896
C_MLA_KR = 1152
C_SWA_V = 1280
C_S5_U = 1408
C_RET_V = 1920
C_RET_G = 2432
C_MLA_CQ = 2944
C_MLA_CKV = 3200
N_IN = 3328


def _dot(a, b):
    return jnp.dot(a, b, preferred_element_type=F32)


def _dot_nt(a, b):
    return lax.dot_general(a, b, (((1,), (1,)), ((), ())), preferred_element_type=F32)


def _dot_tn(a, b):
    return lax.dot_general(a, b, (((0,), (0,)), ((), ())), preferred_element_type=F32)


def _rms(x, g):
    return x * lax.rsqrt(jnp.mean(x * x, axis=-1, keepdims=True) + EPS) * g


def _sigmoid(x):
    return 1.0 / (1.0 + jnp.exp(-x))


def _gelu_tanh(x):
    return x * (0.5 * (1.0 + jnp.tanh(math.sqrt(2.0 / math.pi) * (x + 0.044715 * (x * x * x)))))


def _const_spec(shape):
    zeros = (0,) * len(shape)
    return pl.BlockSpec(shape, lambda *_: zeros)


def _rope_chunks(z, tabs, shift):
    cos, sin_hi, sin_lo = tabs
    out = []
    for c in range(z.shape[1] // LANES):
        zc = z[:, c * LANES:(c + 1) * LANES]
        out.append(zc * cos + pltpu.roll(zc, shift, 1) * sin_hi + pltpu.roll(zc, LANES - shift, 1) * sin_lo)
    return out


def _inproj_kernel(x_ref, g_ref, w_ref, c64_ref, h64_ref, l64_ref, cr_ref, hr_ref, lr_ref, cq_ref, hq_ref, lq_ref,
                   gq_ref, gkv_ref, wuq_ref, wk_ref, wv_ref,
                   swa_ref, s5_ref, ret_ref, mq_ref, mk_ref, mv_ref):
    x = x_ref[...]
    h = _rms(x, g_ref[...]).astype(BF16)
    tab64 = (c64_ref[...], h64_ref[...], l64_ref[...])

    def plain(c0, width):
        return _dot(h, w_ref[:, c0:c0 + width])

    def roped_store(dst_ref, dst0, c0, width):
        for c, chunk in enumerate(_rope_chunks(plain(c0, width), tab64, HEAD_DIM // 2)):
            dst_ref[:, dst0 + c * LANES:dst0 + (c + 1) * LANES] = chunk.astype(BF16)

    roped_store(swa_ref, 0, C_SWA_Q, 512)
    roped_store(swa_ref, 512, C_SWA_K, 128)
    swa_ref[:, 640:768] = plain(C_SWA_V, 128).astype(BF16)
    s5u = plain(C_S5_U, S5_CHANNELS).astype(BF16)
    for qq in range(S5_QUARTERS):
        s5_ref[qq] = s5u[:, qq * LANES:(qq + 1) * LANES]
    roped_store(ret_ref, 0, C_RET_Q, 256)
    roped_store(ret_ref, 256, C_RET_K, 256)
    ret_ref[:, 512:1024] = plain(C_RET_V, 512).astype(BF16)
    ret_ref[:, 1024:1536] = plain(C_RET_G, 512).astype(BF16)

    tabr = (cr_ref[...], hr_ref[...], lr_ref[...])
    tabq = (cq_ref[...], hq_ref[...], lq_ref[...])
    kr = _rope_chunks(plain(C_MLA_KR, LANES), tabr, MLA_ROPE_DIM // 2)[0].astype(BF16)
    cq = _rms(plain(C_MLA_CQ, MLA_Q_RANK), gq_ref[...]).astype(BF16)
    ckv = _rms(plain(C_MLA_CKV, MLA_KV_RANK), gkv_ref[...]).astype(BF16)
    for c, chunk in enumerate(_rope_chunks(_dot(cq, wuq_ref[...]), tabq, MLA_ROPE_DIM // 2)):
        mq_ref[:, c * LANES:(c + 1) * LANES] = chunk.astype(BF16)
    mk_ref[...] = _dot(jnp.concatenate([ckv, kr], axis=1), wk_ref[...]).astype(BF16)
    mv_ref[...] = _dot(ckv, wv_ref[...]).astype(BF16)


def _inproj(x2, p, tabs, seq, tm):
    t = x2.shape[0]
    nblk = seq // tm
    row = lambda i: (i, 0)
    tab = pl.BlockSpec((tm, LANES), lambda i: (i % nblk, 0))
    in_specs = [
        pl.BlockSpec((tm, D_MODEL), row),
        _const_spec((1, D_MODEL)),
        _const_spec((D_MODEL, N_IN)),
        tab, tab, tab, tab, tab, tab, tab, tab, tab,
        _const_spec((1, MLA_Q_RANK)),
        _const_spec((1, MLA_KV_RANK)),
        _const_spec((MLA_Q_RANK, MLA_HEADS * MLA_HEAD_PAD)),
        _const_spec((MLA_KV_RANK + LANES, MLA_HEADS * MLA_HEAD_PAD)),
        _const_spec((MLA_KV_RANK, MLA_HEADS * MLA_V_DIM)),
    ]
    widths = (768, None, 1536, MLA_HEADS * MLA_HEAD_PAD, MLA_HEADS * MLA_HEAD_PAD, MLA_HEADS * MLA_V_DIM)
    s5_spec = pl.BlockSpec((S5_QUARTERS, tm, LANES), lambda i: (0, i, 0))
    s5_shape = jax.ShapeDtypeStruct((S5_QUARTERS, t, LANES), BF16)
    return pl.pallas_call(
        _inproj_kernel,
        grid=(t // tm,),
        in_specs=in_specs,
        out_specs=[s5_spec if w is None else pl.BlockSpec((tm, w), row) for w in widths],
        out_shape=[s5_shape if w is None else jax.ShapeDtypeStruct((t, w), BF16) for w in widths],
        compiler_params=pltpu.CompilerParams(dimension_semantics=("parallel",),
                                             vmem_limit_bytes=VMEM_LIMIT_BYTES),
        name="inproj",
    )(x2, p["g_pre_mix"], p["w1"], *tabs["rope64"], *tabs["roper"], *tabs["ropeq"],
      p["g_q"], p["g_kv"], p["wuq"], p["wk"], p["wv"])


def _swa_kernel(sink_ref, q_ref, kp_ref, kc_ref, vp_ref, vc_ref, o_ref):
    n = pl.program_id(1)
    blocks = q_ref.shape[1] // BLOCK
    kbuf = jnp.concatenate([kp_ref[0], kc_ref[0]], axis=0)
    vbuf = jnp.concatenate([vp_ref[0], vc_ref[0]], axis=0)
    row = lax.broadcasted_iota(jnp.int32, (BLOCK, 2 * BLOCK), 0)
    col = lax.broadcasted_iota(jnp.int32, (BLOCK, 2 * BLOCK), 1)
    diff = row + BLOCK - col
    band = (diff >= 0) & (diff < BLOCK)
    first_key = jnp.where(n > 0, 0, BLOCK)
    band0 = band & (col >= first_key)
    lane = lax.broadcasted_iota(jnp.int32, (BLOCK, LANES), 1)
    lo = lane < HEAD_DIM
    for blk in range(blocks):
        rows = slice(blk * BLOCK, (blk + 1) * BLOCK)
        kk = kbuf[blk * BLOCK:(blk + 2) * BLOCK]
        vv = vbuf[blk * BLOCK:(blk + 2) * BLOCK]
        ok = band0 if blk == 0 else band
        for j in range(SWA_GROUP):
            q2 = q_ref[0, rows, j * LANES:(j + 1) * LANES]
            outs = []
            for half in range(2):
                qm = jnp.where(lo if half == 0 else jnp.logical_not(lo), q2, jnp.zeros_like(q2))
                s = jnp.where(ok, _dot_nt(qm, kk), NEG_INF)
                sink = sink_ref[j + SWA_GROUP * half]
                m = jnp.maximum(jnp.max(s, axis=-1, keepdims=True), sink)
                pr = jnp.exp(s - m)
                denom = jnp.sum(pr, axis=-1, keepdims=True) + jnp.exp(sink - m)
                outs.append(_dot((pr / denom).astype(BF16), vv))
            o_ref[0, rows, j * LANES:(j + 1) * LANES] = jnp.where(lo, outs[0], outs[1]).astype(BF16)


def _swa(swa3, sinks, tile):
    b, s, _ = swa3.shape
    per = tile // BLOCK
    cur = lambda c: (lambda bi, n: (bi, n, c))
    prev = lambda c: (lambda bi, n: (bi, jnp.maximum(n * per - 1, 0), c))
    return pl.pallas_call(
        _swa_kernel,
        grid=(b, s // tile),
        in_specs=[
            pl.BlockSpec(memory_space=pltpu.SMEM),
            pl.BlockSpec((1, tile, 512), lambda bi, n: (bi, n, 0)),
            pl.BlockSpec((1, BLOCK, LANES), prev(4)), pl.BlockSpec((1, tile, LANES), cur(4)),
            pl.BlockSpec((1, BLOCK, LANES), prev(5)), pl.BlockSpec((1, tile, LANES), cur(5)),
        ],
        out_specs=pl.BlockSpec((1, tile, 512), lambda bi, n: (bi, n, 0)),
        out_shape=jax.ShapeDtypeStruct((b, s, 512), BF16),
        compiler_params=pltpu.CompilerParams(dimension_semantics=("parallel", "parallel")),
        name="swa",
    )(sinks, swa3, swa3, swa3, swa3, swa3)


S5_CHUNK = 8
S5_QUARTERS = S5_CHANNELS // LANES
S5_Q_STATES = S5_STATES // S5_QUARTERS
S5_ROW = S5_CHUNK * LANES
S5_SCAN_LANES = 128


def _s5_kernel(u_ref, wst_ref, mi_ref, vc_ref, apr_ref, api_ref, d_ref, z_ref, x_ref, xb_ref):
    rows = u_ref.shape[2]
    steps = apr_ref.shape[1]
    u = u_ref[0, 0]
    x_ref[...] = _dot(u, wst_ref[0])
    row = lax.broadcasted_iota(jnp.int32, (rows, S5_SCAN_LANES), 0)
    for c in range(S5_Q_STATES // S5_SCAN_LANES):
        sl_re = slice(c * S5_SCAN_LANES, (c + 1) * S5_SCAN_LANES)
        sl_im = slice(S5_Q_STATES + c * S5_SCAN_LANES, S5_Q_STATES + (c + 1) * S5_SCAN_LANES)
        re = x_ref[:, sl_re]
        im = x_ref[:, sl_im]
        for k in range(steps):
            d = 1 << k
            ar = apr_ref[0, k:k + 1, sl_re]
            ai = api_ref[0, k:k + 1, sl_re]
            keep = row >= d
            sr = jnp.where(keep, pltpu.roll(re, d, 0), 0.0)
            si = jnp.where(keep, pltpu.roll(im, d, 0), 0.0)
            re, im = re + (ar * sr - ai * si), im + (ar * si + ai * sr)
        keep = row >= 1
        xb_ref[:, sl_re] = jnp.where(keep, pltpu.roll(re, 1, 0), 0.0).astype(BF16)
        xb_ref[:, sl_im] = jnp.where(keep, pltpu.roll(im, 1, 0), 0.0).astype(BF16)
    y = _dot(u, mi_ref[0]) + _dot(xb_ref[...], vc_ref[0]) + jnp.tile(d_ref[0], (1, S5_CHUNK)) * u.astype(F32)
    z_ref[0, 0] = _gelu_tanh(y).astype(BF16)


def _s5(u4, p):
    nq, b, rows, _ = u4.shape
    steps = int(math.log2(rows))
    assert 1 << steps == rows
    blk = pl.BlockSpec((1, 1, rows, S5_ROW), lambda q, bi: (q, bi, 0, 0))
    wspec = pl.BlockSpec((1, S5_ROW, S5_ROW), lambda q, bi: (q, 0, 0))
    pspec = pl.BlockSpec((1, steps, S5_Q_STATES), lambda q, bi: (q, 0, 0))
    return pl.pallas_call(
        _s5_kernel,
        grid=(nq, b),
        in_specs=[blk, wspec, wspec, wspec, pspec, pspec,
                  pl.BlockSpec((1, 1, LANES), lambda q, bi: (q, 0, 0))],
        out_specs=blk,
        out_shape=jax.ShapeDtypeStruct(u4.shape, BF16),
        scratch_shapes=[pltpu.VMEM((rows, S5_ROW), F32), pltpu.VMEM((rows, S5_ROW), BF16)],
        compiler_params=pltpu.CompilerParams(dimension_semantics=("parallel", "parallel"),
                                             vmem_limit_bytes=VMEM_LIMIT_BYTES),
        name="s5",
    )(u4, p["s5_wst"], p["s5_mi"], p["s5_vc"], p["s5_apr"][:, :steps], p["s5_api"][:, :steps], p["s5_d"])


def _ret_kernel(qkvg_q_ref, qkvg_k_ref, v_ref, g_ref, decay_ref, qw_ref, kw_ref, o_ref, state_ref, *, chunk_decay):
    @pl.when(pl.program_id(1) == 0)
    def _():
        state_ref[...] = jnp.zeros_like(state_ref)

    tm = v_ref.shape[1]
    lane = lax.broadcasted_iota(jnp.int32, (RET_CHUNK, LANES), 1)
    lo = lane < RET_QK_DIM
    for c in range(tm // RET_CHUNK):
        rows = slice(c * RET_CHUNK, (c + 1) * RET_CHUNK)
        for hd in range(RET_HEADS):
            pair, half = divmod(hd, 2)
            sel = lo if half == 0 else jnp.logical_not(lo)
            q2 = qkvg_q_ref[0, rows, pair * LANES:(pair + 1) * LANES]
            k2 = qkvg_k_ref[0, rows, pair * LANES:(pair + 1) * LANES]
            qm = jnp.where(sel, q2, jnp.zeros_like(q2))
            km = jnp.where(sel, k2, jnp.zeros_like(k2))
            v = v_ref[0, rows, hd * RET_V_DIM:(hd + 1) * RET_V_DIM]
            st = state_ref[hd]
            s = _dot_nt(qm, km) * decay_ref[hd]
            y = _dot(s.astype(BF16), v) + _dot(qm, st.astype(BF16)) * qw_ref[hd]
            mu = jnp.mean(y, axis=-1, keepdims=True)
            yc = y - mu
            var = jnp.mean(yc * yc, axis=-1, keepdims=True)
            yn = yc * lax.rsqrt(var + EPS)
            g = g_ref[0, rows, hd * RET_V_DIM:(hd + 1) * RET_V_DIM].astype(F32)
            o_ref[0, rows, hd * RET_V_DIM:(hd + 1) * RET_V_DIM] = (g * _sigmoid(g) * yn).astype(BF16)
            kd = (km.astype(F32) * kw_ref[hd]).astype(BF16)
            state_ref[hd] = st * chunk_decay[hd] + _dot_tn(kd, v)


def _ret(ret3, tabs, tm):
    b, s, _ = ret3.shape
    log_gamma = [math.log1p(-(2.0 ** (-5.0 - h))) for h in range(RET_HEADS)]
    chunk_decay = tuple(math.exp(lg * RET_CHUNK) for lg in log_gamma)
    blk = lambda width, c: pl.BlockSpec((1, tm, width), lambda bi, t: (bi, t, c))
    tab_spec = _const_spec((RET_HEADS, RET_CHUNK, LANES))
    return pl.pallas_call(
        functools.partial(_ret_kernel, chunk_decay=chunk_decay),
        grid=(b, s // tm),
        in_specs=[blk(256, 0), blk(256, 1), blk(512, 1), blk(512, 2), tab_spec, tab_spec, tab_spec],
        out_specs=pl.BlockSpec((1, tm, 512), lambda bi, t: (bi, t, 0)),
        out_shape=jax.ShapeDtypeStruct((b, s, 512), BF16),
        scratch_shapes=[pltpu.VMEM((RET_HEADS, LANES, RET_V_DIM), F32)],
        compiler_params=pltpu.CompilerParams(dimension_semantics=("parallel", "arbitrary")),
        name="retention",
    )(ret3, ret3, ret3, ret3, tabs["ret_decay"], tabs["ret_qw"], tabs["ret_kw"])


MLA_HEADS_PER_STEP = 4
def _mla_kernel(q_ref, k_ref, v_ref, o_ref, m_ref, l_ref, acc_ref):
    tq = q_ref.shape[1]
    i = pl.program_id(2)
    m_ref[...] = jnp.full(m_ref.shape, NEG_INF, F32)
    l_ref[...] = jnp.zeros(l_ref.shape, F32)
    acc_ref[...] = jnp.zeros(acc_ref.shape, F32)

    def tile(j, masked):
        start = pl.multiple_of(j * tq, tq)
        for hd in range(MLA_HEADS_PER_STEP):
            pair = hd // 2
            v = v_ref[0, pl.ds(start, tq), pair * LANES:(pair + 1) * LANES]
            q = q_ref[0, :, hd * MLA_HEAD_PAD:(hd + 1) * MLA_HEAD_PAD]
            k = k_ref[0, pl.ds(start, tq), hd * MLA_HEAD_PAD:(hd + 1) * MLA_HEAD_PAD]
            s = _dot_nt(q, k)
            if masked:
                row = lax.broadcasted_iota(jnp.int32, (tq, tq), 0)
                col = lax.broadcasted_iota(jnp.int32, (tq, tq), 1)
                s = jnp.where(col <= row, s, NEG_INF)
            blocks = tq // LANES
            m_prev = m_ref[hd]
            s_max = functools.reduce(jnp.maximum, [s[:, c * LANES:(c + 1) * LANES] for c in range(blocks)])
            m_next = jnp.maximum(m_prev, jnp.max(s_max, axis=1)[:, None])
            pr = jnp.exp2(s - jnp.tile(m_next, (1, blocks)))
            alpha = jnp.exp2(m_prev - m_next)
            p_sum = functools.reduce(jnp.add, [pr[:, c * LANES:(c + 1) * LANES] for c in range(blocks)])
            l_ref[hd] = alpha * l_ref[hd] + jnp.sum(p_sum, axis=1)[:, None]
            m_ref[hd] = m_next
            acc_ref[hd] = alpha * acc_ref[hd] + _dot(pr.astype(BF16), v)

    def body(j, carry):
        tile(j, False)
        return carry

    lax.fori_loop(0, i, body, 0)
    tile(i, True)
    lane = lax.broadcasted_iota(jnp.int32, (tq, LANES), 1)
    for pair in range(MLA_HEADS_PER_STEP // 2):
        o0 = acc_ref[2 * pair] / l_ref[2 * pair]
        o1 = acc_ref[2 * pair + 1] / l_ref[2 * pair + 1]
        o_ref[0, :, pair * LANES:(pair + 1) * LANES] = jnp.where(lane < MLA_V_DIM, o0, o1).astype(BF16)


def _mla(q3, k3, v3, tq):
    b, s, _ = q3.shape
    hps = MLA_HEADS_PER_STEP
    stat = pltpu.VMEM((hps, tq, LANES), F32)
    return pl.pallas_call(
        _mla_kernel,
        grid=(b, MLA_HEADS // hps, s // tq),
        in_specs=[
            pl.BlockSpec((1, tq, hps * MLA_HEAD_PAD), lambda bi, hg, i: (bi, i, hg)),
            pl.BlockSpec((1, s, hps * MLA_HEAD_PAD), lambda bi, hg, i: (bi, 0, hg)),
            pl.BlockSpec((1, s, hps * MLA_V_DIM), lambda bi, hg, i: (bi, 0, hg)),
        ],
        out_specs=pl.BlockSpec((1, tq, hps * MLA_V_DIM), lambda bi, hg, i: (bi, i, hg)),
        out_shape=jax.ShapeDtypeStruct((b, s, MLA_HEADS * MLA_V_DIM), BF16),
        scratch_shapes=[stat, stat, stat],
        compiler_params=pltpu.CompilerParams(dimension_semantics=("parallel", "parallel", "arbitrary"),
                                             vmem_limit_bytes=VMEM_LIMIT_BYTES),
        name="mla",
    )(q3, k3, v3)


def _merge_kernel(x_ref, ya_ref, zb_ref, yc_ref, yd_ref, gpre_ref, wg_ref, wglu_ref, wb_ref, wo_ref, gpost_ref,
                  o_ref):
    x = x_ref[...]
    h = _rms(x, gpre_ref[...]).astype(BF16)
    zb = jnp.concatenate([zb_ref[qq] for qq in range(S5_QUARTERS)], axis=1)
    zz = _dot(zb, wglu_ref[...])
    yb = (zz[:, :S5_CHANNELS] * _sigmoid(zz[:, S5_CHANNELS:])).astype(BF16)
    merged = None
    for br, y in enumerate((ya_ref[...], yb, yc_ref[...], yd_ref[...])):
        gate = _sigmoid(_dot(h, wg_ref[:, br * D_MODEL:(br + 1) * D_MODEL]))
        term = gate * _dot(y, wb_ref[br])
        merged = term if merged is None else merged + term
    o_ref[...] = x + _rms(_dot(merged.astype(BF16), wo_ref[...]), gpost_ref[...])


def _merge(x2, ys, p, tm):
    t = x2.shape[0]
    row = lambda i: (i, 0)
    return pl.pallas_call(
        _merge_kernel,
        grid=(t // tm,),
        in_specs=[pl.BlockSpec((tm, D_MODEL), row),
                  pl.BlockSpec((tm, BRANCH_WIDTH), row),
                  pl.BlockSpec((S5_QUARTERS, tm, LANES), lambda i: (0, i, 0)),
                  pl.BlockSpec((tm, BRANCH_WIDTH), row),
                  pl.BlockSpec((tm, BRANCH_WIDTH), row)]
        + [_const_spec((1, D_MODEL)),
           _const_spec((D_MODEL, N_BRANCH * D_MODEL)),
           _const_spec((S5_CHANNELS, 2 * S5_CHANNELS)),
           _const_spec((N_BRANCH, BRANCH_WIDTH, D_MODEL)),
           _const_spec((D_MODEL, D_MODEL)),
           _const_spec((1, D_MODEL))],
        out_specs=pl.BlockSpec((tm, D_MODEL), row),
        out_shape=jax.ShapeDtypeStruct((t, D_MODEL), F32),
        compiler_params=pltpu.CompilerParams(dimension_semantics=("parallel",),
                                             vmem_limit_bytes=VMEM_LIMIT_BYTES),
        name="merge",
    )(x2, *ys, p["g_pre_mix"], p["w_gate"], p["s5_wglu"], p["w_branch"], p["w_out"], p["g_post_mix"])


FF_CHUNK = 1024


def _mlp_kernel(x_ref, gpre_ref, w1_ref, w2_ref, gpost_ref, o_ref):
    x = x_ref[...]
    h = _rms(x, gpre_ref[...]).astype(BF16)
    f = None
    for c in range(D_FF // FF_CHUNK):
        a = jnp.maximum(_dot(h, w1_ref[:, c * FF_CHUNK:(c + 1) * FF_CHUNK]), 0.0)
        part = _dot((a * a).astype(BF16), w2_ref[c * FF_CHUNK:(c + 1) * FF_CHUNK, :])
        f = part if f is None else f + part
    o_ref[...] = x + _rms(f, gpost_ref[...])


def _mlp(x2, p, tm):
    t = x2.shape[0]
    row = lambda i: (i, 0)
    return pl.pallas_call(
        _mlp_kernel,
        grid=(t // tm,),
        in_specs=[pl.BlockSpec((tm, D_MODEL), row),
                  _const_spec((1, D_MODEL)),
                  _const_spec((D_MODEL, D_FF)),
                  _const_spec((D_FF, D_MODEL)),
                  _const_spec((1, D_MODEL))],
        out_specs=pl.BlockSpec((tm, D_MODEL), row),
        out_shape=jax.ShapeDtypeStruct((t, D_MODEL), F32),
        compiler_params=pltpu.CompilerParams(dimension_semantics=("parallel",),
                                             vmem_limit_bytes=VMEM_LIMIT_BYTES),
        name="mlp",
    )(x2, p["g_pre_mlp"], p["w_ff1"], p["w_ff2"], p["g_post_mlp"])


def _rope_tables(seq):
    def tab(dim):
        inv = 1.0 / (ROPE_THETA ** (jnp.arange(0, dim, 2, dtype=F32) / dim))
        ang = jnp.arange(seq, dtype=F32)[:, None] * inv[None, :]
        return jnp.cos(ang), jnp.sin(ang)

    c64, s64 = tab(HEAD_DIM)
    c32, s32 = tab(MLA_ROPE_DIM)
    zeros = lambda w: jnp.zeros((seq, w), F32)
    ones = lambda w: jnp.ones((seq, w), F32)
    cat = lambda parts: jnp.concatenate(parts, axis=1)
    scale = (MLA_NOPE_DIM + MLA_ROPE_DIM) ** -0.5 * math.log2(math.e)
    half = MLA_ROPE_DIM // 2
    pad = MLA_HEAD_PAD - MLA_NOPE_DIM - MLA_ROPE_DIM
    return {
        "rope64": (cat([c64, c64] * 2), cat([zeros(32), s64] * 2), cat([-s64, zeros(32)] * 2)),
        "roper": (cat([c32, c32, zeros(LANES - MLA_ROPE_DIM)]), cat([zeros(half), s32, zeros(LANES - MLA_ROPE_DIM)]),
                  cat([-s32, zeros(LANES - half)])),
        "ropeq": (scale * cat([ones(MLA_NOPE_DIM), c32, c32, zeros(pad)]),
                  scale * cat([zeros(MLA_NOPE_DIM + half), s32, zeros(pad)]),
                  scale * cat([zeros(MLA_NOPE_DIM), -s32, zeros(half + pad)])),
    }


def _ret_tables():
    h = RET_HEADS
    c = RET_CHUNK
    log_gamma = jnp.log1p(-jnp.exp2(-5.0 - jnp.arange(h, dtype=F32)))
    idx = jnp.arange(c, dtype=F32)
    diff = idx[:, None] - idx[None, :]
    decay = jnp.where(diff >= 0, jnp.exp(log_gamma[:, None, None] * jnp.maximum(diff, 0.0)), 0.0)
    k_w = jnp.exp(log_gamma[:, None] * (c - 1 - idx)[None, :])
    q_w = jnp.exp(log_gamma[:, None] * (idx + 1.0)[None, :])
    bc = lambda a: jnp.broadcast_to(a[:, :, None], (h, c, LANES))
    return {"ret_decay": decay, "ret_qw": bc(q_w), "ret_kw": bc(k_w)}


def _s5_params(lam_re, lam_im, log_dt, b_re, b_im, c_re, c_im, max_steps):
    g = S5_GROUPS
    dt = jnp.exp(log_dt)[:, None]
    mag = jnp.exp(lam_re * dt)
    ab_re, ab_im = mag * jnp.cos(lam_im * dt), mag * jnp.sin(lam_im * dt)
    den = lam_re * lam_re + lam_im * lam_im
    nr, ni = ab_re - 1.0, ab_im
    f_re = (nr * lam_re + ni * lam_im) / den
    f_im = (ni * lam_re - nr * lam_im) / den
    bb_re = f_re[..., None] * b_re - f_im[..., None] * b_im
    bb_im = f_re[..., None] * b_im + f_im[..., None] * b_re
    nq, gl, ch, st, ck = S5_QUARTERS, g // S5_QUARTERS, S5_GROUP, S5_STATE, S5_CHUNK
    pw_re, pw_im = [jnp.ones_like(ab_re)], [jnp.zeros_like(ab_im)]
    for _ in range(ck):
        r, i = pw_re[-1], pw_im[-1]
        pw_re.append(r * ab_re - i * ab_im)
        pw_im.append(r * ab_im + i * ab_re)
    rev = jnp.stack(pw_re[ck - 1::-1]), jnp.stack(pw_im[ck - 1::-1])
    pw_re, pw_im = jnp.stack(pw_re), jnp.stack(pw_im)
    eye = jnp.eye(gl, dtype=F32)
    quarters = lambda v: v.reshape(v.shape[:-3] + (nq, gl) + v.shape[-2:])

    cw_re = rev[0][..., None] * bb_re - rev[1][..., None] * bb_im
    cw_im = rev[0][..., None] * bb_im + rev[1][..., None] * bb_re
    st_mat = lambda v: jnp.einsum("sqgpc,gh->qsgchp", quarters(v), eye).reshape(nq, S5_ROW, gl * st)
    wst = jnp.concatenate([st_mat(cw_re), st_mat(cw_im)], axis=-1)

    ca_re = c_re[None] * pw_re[:, :, None, :] - c_im[None] * pw_im[:, :, None, :]
    ca_im = c_re[None] * pw_im[:, :, None, :] + c_im[None] * pw_re[:, :, None, :]
    kc = jnp.einsum("tgcp,gpd->tgcd", ca_re[:ck], bb_re) - jnp.einsum("tgcp,gpd->tgcd", ca_im[:ck], bb_im)
    tau = jnp.arange(ck)[None, :] - jnp.arange(ck)[:, None]
    kst = jnp.where((tau >= 0)[:, :, None, None, None], kc[jnp.maximum(tau, 0)], 0.0)
    mi = jnp.einsum("stqgcd,gh->qsgdthc", quarters(kst), eye).reshape(nq, S5_ROW, S5_ROW)
    cross = lambda v: jnp.einsum("tqgcp,gh->qgpthc", quarters(v), eye).reshape(nq, gl * st, S5_ROW)
    vc = jnp.concatenate([cross(ca_re[1:]), -cross(ca_im[1:])], axis=1)
    pr, pi = [pw_re[ck].reshape(nq, 1, S5_Q_STATES)], [pw_im[ck].reshape(nq, 1, S5_Q_STATES)]
    for _ in range(max_steps - 1):
        r, i = pr[-1], pi[-1]
        pr.append(r * r - i * i)
        pi.append(2.0 * r * i)
    return (wst.astype(BF16), mi.astype(BF16), vc.astype(BF16),
            jnp.concatenate(pr, axis=1), jnp.concatenate(pi, axis=1))


def _prep_layer(l, a, max_steps):
    w_in = a["w_in"][l]
    offs = np.cumsum([0, 512, 128, 128, 512, 256, 256, 512, 512, 256, 128, 32, 4096])
    (w_sq, w_sk, w_sv, w_su, w_rq, w_rk, w_rv, w_rg, w_cq, w_ckv, w_kr, w_gate) = [
        w_in[:, int(offs[i]):int(offs[i + 1])] for i in range(12)]
    head_order = [h for j in range(SWA_GROUP) for h in (j, SWA_GROUP + j)]
    w_sq = w_sq.reshape(D_MODEL, SWA_HEADS, HEAD_DIM)[:, head_order].reshape(D_MODEL, 512) * (HEAD_DIM ** -0.5)
    w_rk = w_rk * (RET_QK_DIM ** -0.5)
    w_kr_pad = jnp.pad(w_kr, ((0, 0), (0, LANES - MLA_ROPE_DIM)))
    w1 = jnp.concatenate([w_sq, w_sk, w_rq, w_rk, w_kr_pad, w_sv, w_su, w_rv, w_rg, w_cq, w_ckv], axis=1)

    qdim = MLA_NOPE_DIM + MLA_ROPE_DIM
    pad = MLA_HEAD_PAD - qdim
    wuq = a["mla_w_uq"][l].reshape(MLA_Q_RANK, MLA_HEADS, qdim)
    wuq_pad = jnp.pad(wuq, ((0, 0), (0, 0), (0, pad))).reshape(MLA_Q_RANK, MLA_HEADS * MLA_HEAD_PAD)
    wukv = a["mla_w_ukv"][l].reshape(MLA_KV_RANK, MLA_HEADS, MLA_NOPE_DIM + MLA_V_DIM)
    wk_nope = jnp.pad(wukv[:, :, :MLA_NOPE_DIM], ((0, 0), (0, 0), (0, MLA_HEAD_PAD - MLA_NOPE_DIM)))
    place = jnp.zeros((LANES, MLA_HEADS, MLA_HEAD_PAD), F32).at[
        jnp.arange(MLA_ROPE_DIM), :, MLA_NOPE_DIM + jnp.arange(MLA_ROPE_DIM)].set(1.0)
    wk = jnp.concatenate([wk_nope, place], axis=0).reshape(MLA_KV_RANK + LANES, MLA_HEADS * MLA_HEAD_PAD)
    wv = wukv[:, :, MLA_NOPE_DIM:].reshape(MLA_KV_RANK, MLA_HEADS * MLA_V_DIM)

    wst, mi, vc, apr, api = _s5_params(a["s5_lam_re"][l], a["s5_lam_im"][l], a["s5_log_dt"][l], a["s5_b_re"][l],
                                       a["s5_b_im"][l], a["s5_c_re"][l], a["s5_c_im"][l], max_steps)
    wb = a["w_branch"][l]
    wb0 = wb[0].reshape(SWA_HEADS, HEAD_DIM, D_MODEL)[jnp.array(head_order)].reshape(BRANCH_WIDTH, D_MODEL)
    row = lambda v: v.reshape(1, -1).astype(F32)
    return {
        "g_pre_mix": row(a["g_pre_mix"][l]), "g_post_mix": row(a["g_post_mix"][l]),
        "g_pre_mlp": row(a["g_pre_mlp"][l]), "g_post_mlp": row(a["g_post_mlp"][l]),
        "w1": w1.astype(BF16),
        "g_q": row(a["mla_g_q"][l]), "g_kv": row(a["mla_g_kv"][l]),
        "wuq": wuq_pad.astype(BF16), "wk": wk.astype(BF16), "wv": wv.astype(BF16),
        "sinks": a["swa_sinks"][l].astype(F32),
        "s5_wst": wst, "s5_mi": mi, "s5_vc": vc, "s5_apr": apr, "s5_api": api,
        "s5_d": a["s5_d"][l].reshape(S5_QUARTERS, 1, LANES).astype(F32), "s5_wglu": a["s5_w_glu"][l].astype(BF16),
        "w_gate": w_gate.astype(BF16),
        "w_branch": jnp.concatenate([wb0[None], wb[1:]], axis=0).astype(BF16),
        "w_out": a["w_out"][l].astype(BF16),
        "w_ff1": a["w_ff1"][l].astype(BF16), "w_ff2": a["w_ff2"][l].astype(BF16),
    }


def _tile(n, pref):
    t = min(n, pref)
    assert n % t == 0, (n, t)
    return t


def kernel(x, g_pre_mix, g_post_mix, g_pre_mlp, g_post_mlp, w_in, swa_sinks, s5_lam_re, s5_lam_im, s5_log_dt,
           s5_b_re, s5_b_im, s5_c_re, s5_c_im, s5_d, s5_w_glu, mla_g_q, mla_g_kv, mla_w_uq, mla_w_ukv,
           w_branch, w_out, w_ff1, w_ff2):
    args = dict(g_pre_mix=g_pre_mix, g_post_mix=g_post_mix, g_pre_mlp=g_pre_mlp, g_post_mlp=g_post_mlp,
                w_in=w_in, swa_sinks=swa_sinks, s5_lam_re=s5_lam_re, s5_lam_im=s5_lam_im, s5_log_dt=s5_log_dt,
                s5_b_re=s5_b_re, s5_b_im=s5_b_im, s5_c_re=s5_c_re, s5_c_im=s5_c_im, s5_d=s5_d, s5_w_glu=s5_w_glu,
                mla_g_q=mla_g_q, mla_g_kv=mla_g_kv, mla_w_uq=mla_w_uq, mla_w_ukv=mla_w_ukv,
                w_branch=w_branch, w_out=w_out, w_ff1=w_ff1, w_ff2=w_ff2)
    b, s, d = x.shape
    assert d == D_MODEL and s % BLOCK == 0
    t = b * s
    tm_proj = _tile(s, 512)
    tm_swa = _tile(s, 512)
    tm_ret = _tile(s, 512)
    tq_mla = _tile(s, 512)
    tm_tok = _tile(t, 256)
    s5_rows = s // S5_CHUNK
    tabs = {**_rope_tables(s), **_ret_tables()}
    x2 = x.reshape(t, d)
    for l in range(DEPTH):
        p = _prep_layer(l, args, int(math.log2(s5_rows)))
        swa, s5u, ret, mq, mk, mv = _inproj(x2, p, tabs, s, tm_proj)
        to3 = lambda v: v.reshape(b, s, v.shape[-1])
        y_a = _swa(to3(swa), p["sinks"], tm_swa)
        z_b = _s5(s5u.reshape(S5_QUARTERS, b, s5_rows, S5_ROW), p).reshape(S5_QUARTERS, t, LANES)
        y_c = _ret(to3(ret), tabs, tm_ret)
        y_d = _mla(to3(mq), to3(mk), to3(mv), tq_mla)
        ys = [y_a.reshape(t, BRANCH_WIDTH), z_b, y_c.reshape(t, BRANCH_WIDTH), y_d.reshape(t, BRANCH_WIDTH)]
        x2 = _merge(x2, ys, p, tm_tok)
        x2 = _mlp(x2, p, tm_tok)
    return x2.reshape(b, s, d)
```
